```python
import math
import jax, jax.numpy as jnp
from jax import lax
import numpy as np

D_MODEL = 1024
BATCH = 4
SEQ = 4096
DEPTH = 4

CHUNK = 64
N_MEM = 256
EPS = 1e-6
NEG_INF = -1e30
N_NORMS = 6

BRANCH_DIM = D_MODEL // 2
N_BRANCH = 3

HEAD_DIM = 64
A_Q_HEADS = BRANCH_DIM // HEAD_DIM
A_KV_HEADS = 2
A_GROUP = A_Q_HEADS // A_KV_HEADS
A_Q_DIM = A_Q_HEADS * HEAD_DIM
A_KV_DIM = A_KV_HEADS * HEAD_DIM
WINDOW = 128
WINDOW_CHUNKS = WINDOW // CHUNK
ATT_BLOCK = 128

SGU_CHUNK = 128
SGU_GROUPS = 4
SGU_DIM = BRANCH_DIM
SGU_GROUP_DIM = SGU_DIM // SGU_GROUPS

POOL_WINDOWS = (2, 4, 8, 16)
POOL_GROUPS = 4
POOL_DIM = BRANCH_DIM
POOL_GROUP_DIM = POOL_DIM // POOL_GROUPS

IN_SIZES = (A_Q_DIM, A_KV_DIM, A_KV_DIM, SGU_DIM, SGU_DIM, POOL_DIM, N_BRANCH * D_MODEL)
IN_DIM = A_Q_DIM + 2 * A_KV_DIM + 2 * SGU_DIM + POOL_DIM + N_BRANCH * D_MODEL

MEM_HEADS = 4
MEM_HEAD_DIM = 128
MEM_DIM = MEM_HEADS * MEM_HEAD_DIM

D_FF = 4 * D_MODEL

kernel_name = "hybrid_gated_parallel_streaming_trunk"


def rmsnorm(x, g):
    xf = x.astype(jnp.float32)
    y = xf * lax.rsqrt(jnp.mean(xf * xf, axis=-1, keepdims=True) + EPS)
    return (y * g.astype(jnp.float32)).astype(x.dtype)


def split_columns(proj):
    parts, start = [], 0
    for size in IN_SIZES:
        parts.append(proj[..., start:start + size])
        start += size
    return parts


def window_sink_attention(q, k, v, sinks):
    B, S, _ = q.shape
    nb = S // ATT_BLOCK
    qb = q.reshape(B, nb, ATT_BLOCK, A_KV_HEADS, A_GROUP, HEAD_DIM)

    def band(t):
        t = t.reshape(B, S, A_KV_HEADS, HEAD_DIM)
        t = jnp.pad(t, ((0, 0), (ATT_BLOCK, 0), (0, 0), (0, 0)))
        t = t.reshape(B, nb + 1, ATT_BLOCK, A_KV_HEADS, HEAD_DIM)
        return jnp.concatenate([t[:, :-1], t[:, 1:]], axis=2)

    kb, vb = band(k), band(v)
    s = jnp.einsum('bnqhgd,bnkhd->bnhgqk', qb, kb).astype(jnp.float32) * (1.0 / math.sqrt(HEAD_DIM))

    blk = jnp.arange(nb)[:, None, None]
    qpos = blk * ATT_BLOCK + jnp.arange(ATT_BLOCK)[None, :, None]
    kpos = (blk - 1) * ATT_BLOCK + jnp.arange(2 * ATT_BLOCK)[None, None, :]
    qc, kc = qpos // CHUNK, kpos // CHUNK
    valid = (kpos >= 0) & (kc <= qc) & (kc >= qc - WINDOW_CHUNKS)
    s = jnp.where(valid[None, :, None, None], s, NEG_INF)

    sink = sinks.astype(jnp.float32).reshape(A_KV_HEADS, A_GROUP)[None, None, :, :, None, None]
    m = jnp.maximum(jnp.max(s, axis=-1, keepdims=True), sink)
    p = jnp.exp(s - m)
    p = p / (jnp.sum(p, axis=-1, keepdims=True) + jnp.exp(sink - m))
    o = jnp.einsum('bnhgqk,bnkhd->bnqhgd', p.astype(v.dtype), vb)
    return o.reshape(B, S, A_Q_DIM)


def spatial_gating(u, v, g_sgu, w_s, b_s):
    B, S, _ = u.shape
    nc = S // SGU_CHUNK
    u = jax.nn.gelu(u)
    v = rmsnorm(jax.nn.gelu(v), g_sgu)
    vb = v.reshape(B, nc, SGU_CHUNK, SGU_GROUPS, SGU_GROUP_DIM)
    pc = jnp.arange(SGU_CHUNK) // CHUNK
    mask = pc[None, :] <= pc[:, None]
    w = jnp.where(mask[None], w_s, 0.0).astype(v.dtype)
    sp = jnp.einsum('gij,bnjgc->bnigc', w, vb) + b_s.T[:, :, None].astype(v.dtype)
    return u * sp.reshape(B, S, SGU_DIM)


def multiscale_pool(c, w_pool, pool_scale):
    B, S, _ = c.shape
    cs = jnp.cumsum(c.astype(jnp.float32), axis=1)
    t = jnp.arange(S)
    outs = []
    for gi, w in enumerate(POOL_WINDOWS):
        cg = cs[..., gi * POOL_GROUP_DIM:(gi + 1) * POOL_GROUP_DIM]
        lag = jnp.pad(cg, ((0, 0), (w, 0), (0, 0)))[:, :S]
        cnt = jnp.minimum(t + 1, w).astype(jnp.float32)[None, :, None]
        outs.append((cg - lag) / cnt)
    pooled = jnp.concatenate(outs, axis=-1).astype(c.dtype) - c
    pooled = pooled.reshape(B, S, POOL_GROUPS, POOL_GROUP_DIM)
    mixed = jnp.einsum('bsgc,gcd->bsgd', pooled, w_pool).reshape(B, S, POOL_DIM)
    return mixed * pool_scale


def memory_attention(h, mem_n, w_q, w_kv, w_o):
    B, S, _ = h.shape
    q = (h @ w_q).reshape(B, S, MEM_HEADS, MEM_HEAD_DIM)
    kv = mem_n @ w_kv
    k = kv[..., :MEM_DIM].reshape(B, N_MEM, MEM_HEADS, MEM_HEAD_DIM)
    v = kv[..., MEM_DIM:].reshape(B, N_MEM, MEM_HEADS, MEM_HEAD_DIM)
    s = jnp.einsum('bshd,bmhd->bhsm', q, k).astype(jnp.float32) * (1.0 / math.sqrt(MEM_HEAD_DIM))
    p = jax.nn.softmax(s, axis=-1).astype(h.dtype)
    o = jnp.einsum('bhsm,bmhd->bshd', p, v).reshape(B, S, MEM_DIM)
    return o @ w_o


def setup_inputs(seed: int = 0) -> dict:
    key = jax.random.key(seed)
    ks = jax.random.split(key, 20)
    f32 = jnp.float32

    def dense(k, shape, fan_in):
        return jax.random.normal(k, shape, f32) * (fan_in ** -0.5)

    return {
        "x": jax.random.normal(ks[0], (BATCH, SEQ, D_MODEL), f32),
        "mem": jax.random.normal(ks[1], (BATCH, N_MEM, D_MODEL), f32),
        "g_norm": 1.0 + 0.1 * jax.random.normal(ks[2], (DEPTH, N_NORMS, D_MODEL), f32),
        "g_mem": 1.0 + 0.1 * jax.random.normal(ks[3], (DEPTH, D_MODEL), f32),
        "w_in": dense(ks[4], (DEPTH, D_MODEL, IN_DIM), D_MODEL),
        "attn_sinks": 0.5 * jax.random.normal(ks[5], (DEPTH, A_Q_HEADS), f32),
        "w_spatial": dense(ks[6], (DEPTH, SGU_GROUPS, SGU_CHUNK, SGU_CHUNK), SGU_CHUNK),
        "b_spatial": 1.0 + 0.1 * jax.random.normal(ks[7], (DEPTH, SGU_GROUPS, SGU_CHUNK), f32),
        "g_sgu": 1.0 + 0.1 * jax.random.normal(ks[8], (DEPTH, SGU_DIM), f32),
        "w_pool": dense(ks[9], (DEPTH, POOL_GROUPS, POOL_GROUP_DIM, POOL_GROUP_DIM), POOL_GROUP_DIM),
        "pool_scale": 1.0 + 0.1 * jax.random.normal(ks[10], (DEPTH, POOL_DIM), f32),
        "w_branch": dense(ks[11], (DEPTH, N_BRANCH, BRANCH_DIM, D_MODEL), BRANCH_DIM),
        "w_out": dense(ks[12], (DEPTH, D_MODEL, D_MODEL), D_MODEL),
        "w_q_mem": dense(ks[13], (DEPTH, D_MODEL, MEM_DIM), D_MODEL),
        "w_kv_mem": dense(ks[14], (DEPTH, D_MODEL, 2 * MEM_DIM), D_MODEL),
        "w_o_mem": dense(ks[15], (DEPTH, MEM_DIM, D_MODEL), MEM_DIM),
        "w_up": dense(ks[16], (DEPTH, D_MODEL, D_FF), D_MODEL),
        "w_down": dense(ks[17], (DEPTH, D_FF, D_MODEL), D_FF),
    }


def reference(x, mem, g_norm, g_mem, w_in, attn_sinks, w_spatial, b_spatial, g_sgu,
              w_pool, pool_scale, w_branch, w_out, w_q_mem, w_kv_mem, w_o_mem, w_up, w_down):
    B, S, _ = x.shape
    for l in range(DEPTH):
        h = rmsnorm(x, g_norm[l, 0])
        q, k, v, su, sv, pc, gate = split_columns(h @ w_in[l])
        ya = window_sink_attention(q, k, v, attn_sinks[l])
        yb = spatial_gating(su, sv, g_sgu[l], w_spatial[l], b_spatial[l])
        yc = multiscale_pool(pc, w_pool[l], pool_scale[l])
        branches = jnp.stack([ya, yb, yc], axis=2)
        proj = jnp.einsum('bsnc,ncd->bsnd', branches, w_branch[l])
        gates = jax.nn.sigmoid(gate.reshape(B, S, N_BRANCH, D_MODEL))
        merged = jnp.sum(gates * proj, axis=2)
        x = x + rmsnorm(merged @ w_out[l], g_norm[l, 1])
        hm = rmsnorm(x, g_norm[l, 2])
        mem_n = rmsnorm(mem, g_mem[l])
        ym = memory_attention(hm, mem_n, w_q_mem[l], w_kv_mem[l], w_o_mem[l])
        x = x + rmsnorm(ym, g_norm[l, 3])
        hf = rmsnorm(x, g_norm[l, 4])
        yf = jnp.square(jax.nn.relu(hf @ w_up[l])) @ w_down[l]
        x = x + rmsnorm(yf, g_norm[l, 5])
    return x
```

```python
import functools
import math

import jax
import jax.numpy as jnp
from jax import lax
from jax.experimental import pallas as pl
from jax.experimental.pallas import tpu as pltpu

D_MODEL = 1024
DEPTH = 4
CHUNK = 64
N_MEM = 256
EPS = 1e-6
NEG_INF = -1e30
N_NORMS = 6

BRANCH_DIM = D_MODEL // 2
N_BRANCH = 3
HEAD_DIM = 64
A_Q_HEADS = BRANCH_DIM // HEAD_DIM
A_KV_HEADS = 2
A_GROUP = A_Q_HEADS // A_KV_HEADS
A_Q_DIM = A_Q_HEADS * HEAD_DIM
A_KV_DIM = A_KV_HEADS * HEAD_DIM
ATT_BLOCK = 128
SGU_CHUNK = 128
SGU_GROUPS = 4
SGU_DIM = BRANCH_DIM
SGU_GROUP_DIM = SGU_DIM // SGU_GROUPS
POOL_WINDOWS = (2, 4, 8, 16)
POOL_GROUPS = 4
POOL_DIM = BRANCH_DIM
POOL_GROUP_DIM = POOL_DIM // POOL_GROUPS
POOL_HALO = 16
IN_DIM = A_Q_DIM + 2 * A_KV_DIM + 2 * SGU_DIM + POOL_DIM + N_BRANCH * D_MODEL
MEM_HEADS = 4
MEM_HEAD_DIM = 128
MEM_DIM = MEM_HEADS * MEM_HEAD_DIM
D_FF = 4 * D_MODEL

OFF_Q = 0
OFF_K = OFF_Q + A_Q_DIM
OFF_V = OFF_K + A_KV_DIM
OFF_SU = OFF_V + A_KV_DIM
OFF_SV = OFF_SU + SGU_DIM
OFF_PC = OFF_SV + SGU_DIM
OFF_GATE = OFF_PC + POOL_DIM

LANES = 128
TOKEN_BLOCK = 512
FF_CHUNK = 1024
VMEM_LIMIT_BYTES = 56 * 1024 * 1024

BF16 = jnp.bfloat16
F32 = jnp.float32


def _rms(x, g):
    ms = jnp.mean(x * x, axis=-1, keepdims=True)
    return x * lax.rsqrt(ms + EPS) * g


def _dot(a, b):
    return jnp.dot(a, b, preferred_element_type=F32)


def _dot_nt(a, b):
    return lax.dot_general(a, b, (((1,), (1,)), ((), ())), preferred_element_type=F32)


def _const_spec(shape):
    zeros = (0,) * len(shape)
    return pl.BlockSpec(shape, lambda i: zeros, pipeline_mode=pl.Buffered(1))


def _mixer_kernel(blocks_per_seq, sinks_ref, x_ref, g_ref, w_in_ref, w_sp_ref, b_sp_ref, g_sgu_ref,
                  w_pool_ref, pool_scale_ref, w_branch_ref, w_out_ref, o_ref,
                  kbuf, vbuf, cbuf, ya_ref, yb_ref, yc_ref):
    tb = x_ref.shape[0]
    n_qblk = tb // ATT_BLOCK
    step = pl.program_id(0)
    first = (step % blocks_per_seq) == 0
    seq_pos0 = (step % blocks_per_seq) * tb

    x = x_ref[...]
    hb = _rms(x, g_ref[0:1, :]).astype(BF16)

    qkv = _dot(hb, w_in_ref[:, OFF_Q:OFF_SU])
    q = qkv[:, 0:A_Q_DIM] * (1.0 / math.sqrt(HEAD_DIM))
    k_new = qkv[:, A_Q_DIM:A_Q_DIM + A_KV_DIM]
    v_new = qkv[:, A_Q_DIM + A_KV_DIM:A_Q_DIM + 2 * A_KV_DIM]

    @pl.when(first)
    def _():
        kbuf[0:ATT_BLOCK, :] = jnp.zeros((ATT_BLOCK, A_KV_DIM), F32)
        vbuf[0:ATT_BLOCK, :] = jnp.zeros((ATT_BLOCK, A_KV_DIM), F32)
        cbuf[0:POOL_HALO, :] = jnp.zeros((POOL_HALO, POOL_DIM), F32)

    kbuf[ATT_BLOCK:ATT_BLOCK + tb, :] = k_new
    vbuf[ATT_BLOCK:ATT_BLOCK + tb, :] = v_new
    k_all = kbuf[...]
    v_all = vbuf[...]
    k_sw = pltpu.roll(k_all, HEAD_DIM, 1)
    v_sw = pltpu.roll(v_all, HEAD_DIM, 1)
    lo_kv = lax.broadcasted_iota(jnp.int32, k_all.shape, 1) < HEAD_DIM
    k_dup = (jnp.where(lo_kv, k_all, k_sw).astype(BF16), jnp.where(lo_kv, k_sw, k_all).astype(BF16))
    v_lo = (jnp.where(lo_kv, v_all, 0.0).astype(BF16), jnp.where(lo_kv, v_sw, 0.0).astype(BF16))
    v_hi = (jnp.where(lo_kv, 0.0, v_sw).astype(BF16), jnp.where(lo_kv, 0.0, v_all).astype(BF16))

    lo_q = lax.broadcasted_iota(jnp.int32, (ATT_BLOCK, LANES), 1) < HEAD_DIM
    q_chunk = lax.broadcasted_iota(jnp.int32, (ATT_BLOCK, 2 * ATT_BLOCK), 0) // CHUNK
    k_col = lax.broadcasted_iota(jnp.int32, (ATT_BLOCK, 2 * ATT_BLOCK), 1)
    k_chunk = k_col // CHUNK
    band = (k_chunk >= q_chunk) & (k_chunk <= q_chunk + 2)
    k_min = jnp.where(first, ATT_BLOCK, 0)

    for j in range(n_qblk):
        rows = slice(j * ATT_BLOCK, (j + 1) * ATT_BLOCK)
        keys = slice(j * ATT_BLOCK, (j + 2) * ATT_BLOCK)
        valid = (band & (k_col >= k_min)) if j == 0 else band
        for h in range(A_KV_HEADS):
            q_stack = []
            for p in (2 * h, 2 * h + 1):
                qp = q[rows, p * LANES:(p + 1) * LANES]
                q_stack.append(jnp.where(lo_q, qp, 0.0).astype(BF16))
                q_stack.append(jnp.where(lo_q, 0.0, qp).astype(BF16))
            s_all = _dot_nt(jnp.concatenate(q_stack, axis=0), k_dup[h][keys])
            probs, recip = [], []
            for a in range(A_GROUP):
                sink = sinks_ref[h * A_GROUP + a]
                s = jnp.where(valid, s_all[a * ATT_BLOCK:(a + 1) * ATT_BLOCK], NEG_INF)
                m = jnp.maximum(jnp.max(s, axis=-1, keepdims=True), sink)
                e = jnp.exp(s - m)
                denom = jnp.sum(e, axis=-1, keepdims=True) + jnp.exp(sink - m)
                probs.append(e.astype(BF16))
                recip.append(1.0 / denom)
            for pi in range(2):
                o_pair = _dot(probs[2 * pi], v_lo[h][keys]) + _dot(probs[2 * pi + 1], v_hi[h][keys])
                o_pair = o_pair * jnp.where(lo_q, recip[2 * pi], recip[2 * pi + 1])
                p = 2 * h + pi
                ya_ref[rows, p * LANES:(p + 1) * LANES] = o_pair.astype(BF16)

    kbuf[0:ATT_BLOCK, :] = k_new[tb - ATT_BLOCK:tb, :]
    vbuf[0:ATT_BLOCK, :] = v_new[tb - ATT_BLOCK:tb, :]

    uv = _dot(hb, w_in_ref[:, OFF_SU:OFF_PC])
    u = jax.nn.gelu(uv[:, 0:SGU_DIM])
    vn = _rms(jax.nn.gelu(uv[:, SGU_DIM:2 * SGU_DIM]), g_sgu_ref[...]).astype(BF16)
    sp_row = lax.broadcasted_iota(jnp.int32, (SGU_CHUNK, SGU_CHUNK), 0) // CHUNK
    sp_col = lax.broadcasted_iota(jnp.int32, (SGU_CHUNK, SGU_CHUNK), 1) // CHUNK
    w_sp = [jnp.where(sp_col <= sp_row, w_sp_ref[g], 0.0).astype(BF16) for g in range(SGU_GROUPS)]
    for c in range(tb // SGU_CHUNK):
        rows = slice(c * SGU_CHUNK, (c + 1) * SGU_CHUNK)
        for g in range(SGU_GROUPS):
            cols = slice(g * SGU_GROUP_DIM, (g + 1) * SGU_GROUP_DIM)
            sp = _dot(w_sp[g], vn[rows, cols]) + b_sp_ref[:, cols]
            yb_ref[rows, cols] = (u[rows, cols] * sp).astype(BF16)

    pc = _dot(hb, w_in_ref[:, OFF_PC:OFF_GATE])
    cbuf[POOL_HALO:POOL_HALO + tb, :] = pc
    t_pos = seq_pos0 + lax.broadcasted_iota(jnp.int32, (tb, POOL_GROUP_DIM), 0)
    for g, w in enumerate(POOL_WINDOWS):
        cols = slice(g * POOL_GROUP_DIM, (g + 1) * POOL_GROUP_DIM)
        acc = pc[:, cols]
        for lag in range(1, w):
            acc = acc + cbuf[POOL_HALO - lag:POOL_HALO - lag + tb, cols]
        cnt = jnp.minimum(t_pos + 1, w).astype(F32)
        pooled = (acc / cnt - pc[:, cols]).astype(BF16)
        mixed = _dot(pooled, w_pool_ref[g]) * pool_scale_ref[:, cols]
        yc_ref[:, cols] = mixed.astype(BF16)
    cbuf[0:POOL_HALO, :] = pc[tb - POOL_HALO:tb, :]

    merged = None
    for n, y_ref in enumerate((ya_ref, yb_ref, yc_ref)):
        gate = _dot(hb, w_in_ref[:, OFF_GATE + n * D_MODEL:OFF_GATE + (n + 1) * D_MODEL])
        proj = _dot(y_ref[...], w_branch_ref[n])
        term = (0.5 * jnp.tanh(0.5 * gate) + 0.5) * proj
        merged = term if merged is None else merged + term
    out = _dot(merged.astype(BF16), w_out_ref[...])
    o_ref[...] = x + _rms(out, g_ref[1:2, :])


def _mixer(x, sinks, g, w_in, w_sp, b_sp_full, g_sgu, w_pool, pool_scale, w_branch, w_out, seq_len):
    t, d = x.shape
    tb = TOKEN_BLOCK
    kern = functools.partial(_mixer_kernel, seq_len // tb)
    return pl.pallas_call(
        kern,
        out_shape=jax.ShapeDtypeStruct((t, d), F32),
        grid=(t // tb,),
        in_specs=[
            pl.BlockSpec(memory_space=pltpu.SMEM),
            pl.BlockSpec((tb, d), lambda i: (i, 0)),
            _const_spec(g.shape),
            _const_spec(w_in.shape),
            _const_spec(w_sp.shape),
            _const_spec(b_sp_full.shape),
            _const_spec(g_sgu.shape),
            _const_spec(w_pool.shape),
            _const_spec(pool_scale.shape),
            _const_spec(w_branch.shape),
            _const_spec(w_out.shape),
        ],
        out_specs=pl.BlockSpec((tb, d), lambda i: (i, 0)),
        scratch_shapes=[
            pltpu.VMEM((ATT_BLOCK + tb, A_KV_DIM), F32),
            pltpu.VMEM((ATT_BLOCK + tb, A_KV_DIM), F32),
            pltpu.VMEM((POOL_HALO + tb, POOL_DIM), F32),
            pltpu.VMEM((tb, BRANCH_DIM), BF16),
            pltpu.VMEM((tb, BRANCH_DIM), BF16),
            pltpu.VMEM((tb, BRANCH_DIM), BF16),
        ],
        compiler_params=pltpu.CompilerParams(
            dimension_semantics=("arbitrary",), vmem_limit_bytes=VMEM_LIMIT_BYTES),
        name="mixer",
    )(sinks, x, g, w_in, w_sp, b_sp_full, g_sgu, w_pool, pool_scale, w_branch, w_out)


def _mem_kv_kernel(mem_ref, g_ref, w_ref, k_ref, v_ref):
    mem_n = _rms(mem_ref[0], g_ref[0]).astype(BF16)
    kv = _dot(mem_n, w_ref[0])
    k_ref[0, 0] = kv[:, 0:MEM_DIM].astype(BF16)
    v_ref[0, 0] = kv[:, MEM_DIM:2 * MEM_DIM].astype(BF16)


def _mem_kv(mem, g_mem, w_kv):
    b = mem.shape[0]
    out = jax.ShapeDtypeStruct((DEPTH, b, N_MEM, MEM_DIM), BF16)
    return pl.pallas_call(
        _mem_kv_kernel,
        out_shape=(out, out),
        grid=(DEPTH, b),
        in_specs=[
            pl.BlockSpec((1, N_MEM, D_MODEL), lambda l, i: (i, 0, 0)),
            pl.BlockSpec((1, 1, D_MODEL), lambda l, i: (l, 0, 0)),
            pl.BlockSpec((1, D_MODEL, 2 * MEM_DIM), lambda l, i: (l, 0, 0)),
        ],
        out_specs=(
            pl.BlockSpec((1, 1, N_MEM, MEM_DIM), lambda l, i: (l, i, 0, 0)),
            pl.BlockSpec((1, 1, N_MEM, MEM_DIM), lambda l, i: (l, i, 0, 0)),
        ),
        compiler_params=pltpu.CompilerParams(
            dimension_semantics=("arbitrary", "arbitrary"), vmem_limit_bytes=VMEM_LIMIT_BYTES),
        name="mem_kv",
    )(mem, g_mem, w_kv)


def _mem_attn_kernel(x_ref, g_ref, w_q_ref, k_ref, v_ref, w_o_ref, o_ref, om_ref):
    x = x_ref[...]
    hb = _rms(x, g_ref[2:3, :]).astype(BF16)
    q = (_dot(hb, w_q_ref[...]) * (1.0 / math.sqrt(MEM_HEAD_DIM))).astype(BF16)
    for h in range(MEM_HEADS):
        cols = slice(h * MEM_HEAD_DIM, (h + 1) * MEM_HEAD_DIM)
        s = _dot_nt(q[:, cols], k_ref[0, :, cols])
        m = jnp.max(s, axis=-1, keepdims=True)
        e = jnp.exp(s - m)
        denom = jnp.sum(e, axis=-1, keepdims=True)
        o = _dot(e.astype(BF16), v_ref[0, :, cols]) * (1.0 / denom)
        om_ref[:, cols] = o.astype(BF16)
    ym = _dot(om_ref[...], w_o_ref[...])
    o_ref[...] = x + _rms(ym, g_ref[3:4, :])


def _mem_attn(x, g, w_q, k_mem, v_mem, w_o, seq_len):
    t, d = x.shape
    tb = TOKEN_BLOCK
    bps = seq_len // tb
    return pl.pallas_call(
        _mem_attn_kernel,
        out_shape=jax.ShapeDtypeStruct((t, d), F32),
        grid=(t // tb,),
        in_specs=[
            pl.BlockSpec((tb, d), lambda i: (i, 0)),
            _const_spec(g.shape),
            _const_spec(w_q.shape),
            pl.BlockSpec((1, N_MEM, MEM_DIM), lambda i: (i // bps, 0, 0)),
            pl.BlockSpec((1, N_MEM, MEM_DIM), lambda i: (i // bps, 0, 0)),
            _const_spec(w_o.shape),
        ],
        out_specs=pl.BlockSpec((tb, d), lambda i: (i, 0)),
        scratch_shapes=[pltpu.VMEM((tb, MEM_DIM), BF16)],
        compiler_params=pltpu.CompilerParams(
            dimension_semantics=("arbitrary",), vmem_limit_bytes=VMEM_LIMIT_BYTES),
        name="mem_attn",
    )(x, g, w_q, k_mem, v_mem, w_o)


def _ffn_kernel(x_ref, g_ref, w_up_ref, w_down_ref, o_ref):
    x = x_ref[...]
    hb = _rms(x, g_ref[4:5, :]).astype(BF16)
    yf = None
    for c in range(D_FF // FF_CHUNK):
        cols = slice(c * FF_CHUNK, (c + 1) * FF_CHUNK)
        up = jnp.maximum(_dot(hb, w_up_ref[:, cols]), 0.0)
        part = _dot((up * up).astype(BF16), w_down_ref[cols, :])
        yf = part if yf is None else yf + part
    o_ref[...] = x + _rms(yf, g_ref[5:6, :])


def _ffn(x, g, w_up, w_down):
    t, d = x.shape
    tb = TOKEN_BLOCK
    return pl.pallas_call(
        _ffn_kernel,
        out_shape=jax.ShapeDtypeStruct((t, d), F32),
        grid=(t // tb,),
        in_specs=[
            pl.BlockSpec((tb, d), lambda i: (i, 0)),
            _const_spec(g.shape),
            _const_spec(w_up.shape),
            _const_spec(w_down.shape),
        ],
        out_specs=pl.BlockSpec((tb, d), lambda i: (i, 0)),
        compiler_params=pltpu.CompilerParams(
            dimension_semantics=("arbitrary",), vmem_limit_bytes=VMEM_LIMIT_BYTES),
        name="ffn",
    )(x, g, w_up, w_down)


def kernel(x, mem, g_norm, g_mem, w_in, attn_sinks, w_spatial, b_spatial, g_sgu, w_pool, pool_scale,
           w_branch, w_out, w_q_mem, w_kv_mem, w_o_mem, w_up, w_down):
    b, s, d = x.shape
    assert d == D_MODEL and s % TOKEN_BLOCK == 0 and TOKEN_BLOCK % ATT_BLOCK == 0
    xt = x.reshape(b * s, d)

    k_mem, v_mem = _mem_kv(mem, g_mem.reshape(DEPTH, 1, D_MODEL), w_kv_mem.astype(BF16))
    b_sp_full = jnp.repeat(jnp.swapaxes(b_spatial, 1, 2), SGU_GROUP_DIM, axis=2)

    for l in range(DEPTH):
        xt = _mixer(xt, attn_sinks[l], g_norm[l], w_in[l].astype(BF16), w_spatial[l], b_sp_full[l],
                    g_sgu[l].reshape(1, SGU_DIM), w_pool[l].astype(BF16),
                    pool_scale[l].reshape(1, POOL_DIM), w_branch[l].astype(BF16),
                    w_out[l].astype(BF16), s)
        xt = _mem_attn(xt, g_norm[l], w_q_mem[l].astype(BF16), k_mem[l], v_mem[l],
                       w_o_mem[l].astype(BF16), s)
        xt = _ffn(xt, g_norm[l], w_up[l].astype(BF16), w_down[l].astype(BF16))
    return xt.reshape(b, s, d)
```

```python
import functools
import math

import jax
import jax.numpy as jnp
from jax import lax
from jax.experimental import pallas as pl
from jax.experimental.pallas import tpu as pltpu

D_MODEL = 1024
DEPTH = 4
CHUNK = 64
N_MEM = 256
EPS = 1e-6
NEG_INF = -1e30
N_NORMS = 6

BRANCH_DIM = D_MODEL // 2
N_BRANCH = 3
HEAD_DIM = 64
A_Q_HEADS = BRANCH_DIM // HEAD_DIM
A_KV_HEADS = 2
A_GROUP = A_Q_HEADS // A_KV_HEADS
A_Q_DIM = A_Q_HEADS * HEAD_DIM
A_KV_DIM = A_KV_HEADS * HEAD_DIM
ATT_BLOCK = 128
SGU_CHUNK = 128
SGU_GROUPS = 4
SGU_DIM = BRANCH_DIM
SGU_GROUP_DIM = SGU_DIM // SGU_GROUPS
POOL_WINDOWS = (2, 4, 8, 16)
POOL_GROUPS = 4
POOL_DIM = BRANCH_DIM
POOL_GROUP_DIM = POOL_DIM // POOL_GROUPS
POOL_HALO = 16
IN_DIM = A_Q_DIM + 2 * A_KV_DIM + 2 * SGU_DIM + POOL_DIM + N_BRANCH * D_MODEL
MEM_HEADS = 4
MEM_HEAD_DIM = 128
MEM_DIM = MEM_HEADS * MEM_HEAD_DIM
D_FF = 4 * D_MODEL

OFF_Q = 0
OFF_K = OFF_Q + A_Q_DIM
OFF_V = OFF_K + A_KV_DIM
OFF_SU = OFF_V + A_KV_DIM
OFF_SV = OFF_SU + SGU_DIM
OFF_PC = OFF_SV + SGU_DIM
OFF_GATE = OFF_PC + POOL_DIM

LANES = 128
TOKEN_BLOCK = 512
WIDE_BLOCK = 1024
FF_CHUNK = 1024
FF_PARTS = 4
MEM_PARTS = 2
MEM_LAG = 2
VMEM_LIMIT_BYTES = 56 * 1024 * 1024

BF16 = jnp.bfloat16
F32 = jnp.float32


def _rms(x, g):
    ms = jnp.mean(x * x, axis=-1, keepdims=True)
    return x * lax.rsqrt(ms + EPS) * g


def _dot(a, b):
    return jnp.dot(a, b, preferred_element_type=F32)


def _dot_nt(a, b):
    return lax.dot_general(a, b, (((1,), (1,)), ((), ())), preferred_element_type=F32)


def _layer_spec(stacked, layer):
    tail = (0,) * (stacked.ndim - 1)
    return pl.BlockSpec((None,) + stacked.shape[1:], lambda i: (layer,) + tail,
                        pipeline_mode=pl.Buffered(1))


def _mixer_kernel(blocks_per_seq, sinks_ref, x_ref, g_ref, w_in_ref, w_sp_ref, b_sp_ref, g_sgu_ref,
                  w_pool_ref, pool_scale_ref, w_branch_ref, w_out_ref, o_ref,
                  kbuf, vbuf, cbuf, ya_ref, yb_ref, yc_ref):
    tb = x_ref.shape[0]
    n_qblk = tb // ATT_BLOCK
    step = pl.program_id(0)
    first = (step % blocks_per_seq) == 0
    seq_pos0 = (step % blocks_per_seq) * tb
    half = tb // 2
    gate_cols = D_MODEL // n_qblk

    @pl.when(first)
    def _():
        kbuf[0:ATT_BLOCK, :] = jnp.zeros((ATT_BLOCK, A_KV_DIM), F32)
        vbuf[0:ATT_BLOCK, :] = jnp.zeros((ATT_BLOCK, A_KV_DIM), F32)
        cbuf[0:POOL_HALO, :] = jnp.zeros((POOL_HALO, POOL_DIM), F32)

    hb = jnp.concatenate(
        [_rms(x_ref[r * half:(r + 1) * half, :], g_ref[0:1, :]).astype(BF16) for r in range(2)], axis=0)

    qkv = _dot(hb, w_in_ref[:, OFF_Q:OFF_SU])
    pc = _dot(hb, w_in_ref[:, OFF_PC:OFF_GATE])

    q = qkv[:, 0:A_Q_DIM] * (1.0 / math.sqrt(HEAD_DIM))
    k_new = qkv[:, A_Q_DIM:A_Q_DIM + A_KV_DIM]
    v_new = qkv[:, A_Q_DIM + A_KV_DIM:A_Q_DIM + 2 * A_KV_DIM]
    kbuf[ATT_BLOCK:ATT_BLOCK + tb, :] = k_new
    vbuf[ATT_BLOCK:ATT_BLOCK + tb, :] = v_new
    k_all = kbuf[...]
    v_all = vbuf[...]
    k_sw = pltpu.roll(k_all, HEAD_DIM, 1)
    v_sw = pltpu.roll(v_all, HEAD_DIM, 1)
    lo_kv = lax.broadcasted_iota(jnp.int32, k_all.shape, 1) < HEAD_DIM
    k_dup = (jnp.where(lo_kv, k_all, k_sw).astype(BF16), jnp.where(lo_kv, k_sw, k_all).astype(BF16))
    v_lo = (jnp.where(lo_kv, v_all, 0.0).astype(BF16), jnp.where(lo_kv, v_sw, 0.0).astype(BF16))
    v_hi = (jnp.where(lo_kv, 0.0, v_sw).astype(BF16), jnp.where(lo_kv, 0.0, v_all).astype(BF16))
    kbuf[0:ATT_BLOCK, :] = k_new[tb - ATT_BLOCK:tb, :]
    vbuf[0:ATT_BLOCK, :] = v_new[tb - ATT_BLOCK:tb, :]

    lo_q = lax.broadcasted_iota(jnp.int32, (ATT_BLOCK, LANES), 1) < HEAD_DIM
    q_chunk = lax.broadcasted_iota(jnp.int32, (ATT_BLOCK, 2 * ATT_BLOCK), 0) // CHUNK
    k_col = lax.broadcasted_iota(jnp.int32, (ATT_BLOCK, 2 * ATT_BLOCK), 1)
    k_chunk = k_col // CHUNK
    band = (k_chunk >= q_chunk) & (k_chunk <= q_chunk + 2)
    k_min = jnp.where(first, ATT_BLOCK, 0)

    def scores(j):
        rows = slice(j * ATT_BLOCK, (j + 1) * ATT_BLOCK)
        keys = slice(j * ATT_BLOCK, (j + 2) * ATT_BLOCK)
        out = []
        for h in range(A_KV_HEADS):
            q_stack = []
            for p in (2 * h, 2 * h + 1):
                qp = q[rows, p * LANES:(p + 1) * LANES]
                q_stack.append(jnp.where(lo_q, qp, 0.0).astype(BF16))
                q_stack.append(jnp.where(lo_q, 0.0, qp).astype(BF16))
            out.append(_dot_nt(jnp.concatenate(q_stack, axis=0), k_dup[h][keys]))
        return out

    def attend(j, s_heads):
        rows = slice(j * ATT_BLOCK, (j + 1) * ATT_BLOCK)
        keys = slice(j * ATT_BLOCK, (j + 2) * ATT_BLOCK)
        valid = (band & (k_col >= k_min)) if j == 0 else band
        for h in range(A_KV_HEADS):
            probs, recip = [], []
            for a in range(A_GROUP):
                sink = sinks_ref[h * A_GROUP + a]
                s = jnp.where(valid, s_heads[h][a * ATT_BLOCK:(a + 1) * ATT_BLOCK], NEG_INF)
                m = jnp.maximum(jnp.max(s, axis=-1, keepdims=True), sink)
                e = jnp.exp(s - m)
                denom = jnp.sum(e, axis=-1, keepdims=True) + jnp.exp(sink - m)
                probs.append(e.astype(BF16))
                recip.append(1.0 / denom)
            for pi in range(2):
                o_pair = _dot(probs[2 * pi], v_lo[h][keys]) + _dot(probs[2 * pi + 1], v_hi[h][keys])
                o_pair = o_pair * jnp.where(lo_q, recip[2 * pi], recip[2 * pi + 1])
                p = 2 * h + pi
                ya_ref[rows, p * LANES:(p + 1) * LANES] = o_pair.astype(BF16)

    s_cur = scores(0)
    uv = _dot(hb, w_in_ref[:, OFF_SU:OFF_PC])
    gate0 = []
    for j in range(n_qblk):
        s_next = scores(j + 1) if j + 1 < n_qblk else None
        gate0.append(_dot(hb, w_in_ref[:, OFF_GATE + j * gate_cols:OFF_GATE + (j + 1) * gate_cols]))
        attend(j, s_cur)
        s_cur = s_next
    gates = [jnp.concatenate(gate0, axis=1)]

    gates.append(_dot(hb, w_in_ref[:, OFF_GATE + D_MODEL:OFF_GATE + 2 * D_MODEL]))
    u = jax.nn.gelu(uv[:, 0:SGU_DIM])
    vn = _rms(jax.nn.gelu(uv[:, SGU_DIM:2 * SGU_DIM]), g_sgu_ref[...]).astype(BF16)
    sp_row = lax.broadcasted_iota(jnp.int32, (SGU_CHUNK, SGU_CHUNK), 0) // CHUNK
    sp_col = lax.broadcasted_iota(jnp.int32, (SGU_CHUNK, SGU_CHUNK), 1) // CHUNK
    w_sp = [jnp.where(sp_col <= sp_row, w_sp_ref[g], 0.0).astype(BF16) for g in range(SGU_GROUPS)]
    for c in range(tb // SGU_CHUNK):
        rows = slice(c * SGU_CHUNK, (c + 1) * SGU_CHUNK)
        for g in range(SGU_GROUPS):
            cols = slice(g * SGU_GROUP_DIM, (g + 1) * SGU_GROUP_DIM)
            sp = _dot(w_sp[g], vn[rows, cols]) + b_sp_ref[:, cols]
            yb_ref[rows, cols] = (u[rows, cols] * sp).astype(BF16)

    gates.append(_dot(hb, w_in_ref[:, OFF_GATE + 2 * D_MODEL:OFF_GATE + 3 * D_MODEL]))
    cbuf[POOL_HALO:POOL_HALO + tb, :] = pc
    t_pos = seq_pos0 + lax.broadcasted_iota(jnp.int32, (tb, POOL_GROUP_DIM), 0)
    for g, w in enumerate(POOL_WINDOWS):
        cols = slice(g * POOL_GROUP_DIM, (g + 1) * POOL_GROUP_DIM)
        acc = cbuf[:, cols]
        lag = 1
        while lag < w:
            acc = acc + pltpu.roll(acc, lag, 0)
            lag *= 2
        cnt = jnp.minimum(t_pos + 1, w).astype(F32)
        pooled = (acc[POOL_HALO:, :] / cnt - pc[:, cols]).astype(BF16)
        mixed = _dot(pooled, w_pool_ref[g]) * pool_scale_ref[:, cols]
        yc_ref[:, cols] = mixed.astype(BF16)
    cbuf[0:POOL_HALO, :] = pc[tb - POOL_HALO:tb, :]

    projs = [_dot(y_ref[...], w_branch_ref[n]) for n, y_ref in enumerate((ya_ref, yb_ref, yc_ref))]
    for r in range(2):
        rows = slice(r * half, (r + 1) * half)
        merged = None
        for gate, proj in zip(gates, projs):
            term = (0.5 * jnp.tanh(0.5 * gate[rows]) + 0.5) * proj[rows]
            merged = term if merged is None else merged + term
        out = _dot(merged.astype(BF16), w_out_ref[...])
        o_ref[rows, :] = x_ref[rows, :] + _rms(out, g_ref[1:2, :])


def _mixer(layer, x, sinks, g_norm, w_in, w_sp, b_sp_full, g_sgu, w_pool, pool_scale, w_branch, w_out,
           seq_len):
    t, d = x.shape
    tb = TOKEN_BLOCK
    assert D_MODEL % (tb // ATT_BLOCK) == 0 and tb % 2 == 0
    kern = functools.partial(_mixer_kernel, seq_len // tb)
    stacked = (g_norm, w_in, w_sp, b_sp_full, g_sgu, w_pool, pool_scale, w_branch, w_out)
    return pl.pallas_call(
        kern,
        out_shape=jax.ShapeDtypeStruct((t, d), F32),
        grid=(t // tb,),
        in_specs=[pl.BlockSpec(memory_space=pltpu.SMEM), pl.BlockSpec((tb, d), lambda i: (i, 0))]
        + [_layer_spec(a, layer) for a in stacked],
        out_specs=pl.BlockSpec((tb, d), lambda i: (i, 0)),
        scratch_shapes=[
            pltpu.VMEM((ATT_BLOCK + tb, A_KV_DIM), F32),
            pltpu.VMEM((ATT_BLOCK + tb, A_KV_DIM), F32),
            pltpu.VMEM((POOL_HALO + tb, POOL_DIM), F32),
            pltpu.VMEM((tb, BRANCH_DIM), BF16),
            pltpu.VMEM((tb, BRANCH_DIM), BF16),
            pltpu.VMEM((tb, BRANCH_DIM), BF16),
        ],
        compiler_params=pltpu.CompilerParams(
            dimension_semantics=("arbitrary",), vmem_limit_bytes=VMEM_LIMIT_BYTES),
        name="mixer",
    )(sinks, x, *stacked)


def _mem_kv_kernel(mem_ref, g_ref, w_ref, k_ref, v_ref):
    mem_n = _rms(mem_ref[0], g_ref[0]).astype(BF16)
    kv = _dot(mem_n, w_ref[0])
    k_ref[0, 0] = kv[:, 0:MEM_DIM].astype(BF16)
    v_ref[0, 0] = kv[:, MEM_DIM:2 * MEM_DIM].astype(BF16)


def _mem_kv(mem, g_mem, w_kv):
    b = mem.shape[0]
    out = jax.ShapeDtypeStruct((DEPTH, b, N_MEM, MEM_DIM), BF16)
    return pl.pallas_call(
        _mem_kv_kernel,
        out_shape=(out, out),
        grid=(DEPTH, b),
        in_specs=[
            pl.BlockSpec((1, N_MEM, D_MODEL), lambda l, i: (i, 0, 0)),
            pl.BlockSpec((1, 1, D_MODEL), lambda l, i: (l, 0, 0)),
            pl.BlockSpec((1, D_MODEL, 2 * MEM_DIM), lambda l, i: (l, 0, 0)),
        ],
        out_specs=(
            pl.BlockSpec((1, 1, N_MEM, MEM_DIM), lambda l, i: (l, i, 0, 0)),
            pl.BlockSpec((1, 1, N_MEM, MEM_DIM), lambda l, i: (l, i, 0, 0)),
        ),
        compiler_params=pltpu.CompilerParams(
            dimension_semantics=("arbitrary", "arbitrary"), vmem_limit_bytes=VMEM_LIMIT_BYTES),
        name="mem_kv",
    )(mem, g_mem, w_kv)


def _mem_attn_kernel(x_ref, g_ref, w_q_ref, k_ref, v_ref, w_o_ref, o_ref, om_ref):
    rows_per_part = x_ref.shape[0] // MEM_PARTS
    parts = [slice(p * rows_per_part, (p + 1) * rows_per_part) for p in range(MEM_PARTS)]

    def query(rows):
        hb = _rms(x_ref[rows, :], g_ref[2:3, :]).astype(BF16)
        return (_dot(hb, w_q_ref[...]) * (1.0 / math.sqrt(MEM_HEAD_DIM))).astype(BF16)

    def scores(q):
        return [_dot_nt(q[:, h * MEM_HEAD_DIM:(h + 1) * MEM_HEAD_DIM],
                        k_ref[:, h * MEM_HEAD_DIM:(h + 1) * MEM_HEAD_DIM]) for h in range(MEM_HEADS)]

    def attend(rows, s_heads):
        for h, s in enumerate(s_heads):
            cols = slice(h * MEM_HEAD_DIM, (h + 1) * MEM_HEAD_DIM)
            m = jnp.max(s, axis=-1, keepdims=True)
            e = jnp.exp(s - m)
            denom = jnp.sum(e, axis=-1, keepdims=True)
            o = _dot(e.astype(BF16), v_ref[:, cols]) * (1.0 / denom)
            om_ref[rows, cols] = o.astype(BF16)
        ym = _dot(om_ref[rows, :], w_o_ref[...])
        o_ref[rows, :] = x_ref[rows, :] + _rms(ym, g_ref[3:4, :])

    pending = []
    for t in range(MEM_PARTS + MEM_LAG):
        if t < MEM_PARTS:
            pending.append(scores(query(parts[t])))
        if t >= MEM_LAG:
            attend(parts[t - MEM_LAG], pending[t - MEM_LAG])


def _mem_attn(layer, x, g_norm, w_q, k_mem, v_mem, w_o, seq_len):
    t, d = x.shape
    tb = WIDE_BLOCK
    bps = seq_len // tb
    kv_spec = pl.BlockSpec((None, None, N_MEM, MEM_DIM), lambda i: (layer, i // bps, 0, 0))
    return pl.pallas_call(
        _mem_attn_kernel,
        out_shape=jax.ShapeDtypeStruct((t, d), F32),
        grid=(t // tb,),
        in_specs=[
            pl.BlockSpec((tb, d), lambda i: (i, 0)),
            _layer_spec(g_norm, layer),
            _layer_spec(w_q, layer),
            kv_spec,
            kv_spec,
            _layer_spec(w_o, layer),
        ],
        out_specs=pl.BlockSpec((tb, d), lambda i: (i, 0)),
        scratch_shapes=[pltpu.VMEM((tb, MEM_DIM), BF16)],
        compiler_params=pltpu.CompilerParams(
            dimension_semantics=("arbitrary",), vmem_limit_bytes=VMEM_LIMIT_BYTES),
        name="mem_attn",
    )(x, g_norm, w_q, k_mem, v_mem, w_o)


def _ffn_kernel(x_ref, g_ref, w_up_ref, w_down_ref, o_ref):
    rows_per_part = x_ref.shape[0] // FF_PARTS
    parts = [slice(p * rows_per_part, (p + 1) * rows_per_part) for p in range(FF_PARTS)]
    hbs, yfs = [], [None] * FF_PARTS
    for c in range(D_FF // FF_CHUNK):
        cols = slice(c * FF_CHUNK, (c + 1) * FF_CHUNK)
        for p, rows in enumerate(parts):
            if c == 0:
                hbs.append(_rms(x_ref[rows, :], g_ref[4:5, :]).astype(BF16))
            up = jnp.maximum(_dot(hbs[p], w_up_ref[:, cols]), 0.0)
            part = _dot((up * up).astype(BF16), w_down_ref[cols, :])
            yfs[p] = part if yfs[p] is None else yfs[p] + part
    for p, rows in enumerate(parts):
        o_ref[rows, :] = x_ref[rows, :] + _rms(yfs[p], g_ref[5:6, :])


def _ffn(layer, x, g_norm, w_up, w_down):
    t, d = x.shape
    tb = WIDE_BLOCK
    return pl.pallas_call(
        _ffn_kernel,
        out_shape=jax.ShapeDtypeStruct((t, d), F32),
        grid=(t // tb,),
        in_specs=[
            pl.BlockSpec((tb, d), lambda i: (i, 0)),
            _layer_spec(g_norm, layer),
            _layer_spec(w_up, layer),
            _layer_spec(w_down, layer),
        ],
        out_specs=pl.BlockSpec((tb, d), lambda i: (i, 0)),
        compiler_params=pltpu.CompilerParams(
            dimension_semantics=("arbitrary",), vmem_limit_bytes=VMEM_LIMIT_BYTES),
        name="ffn",
    )(x, g_norm, w_up, w_down)


def kernel(x, mem, g_norm, g_mem, w_in, attn_sinks, w_spatial, b_spatial, g_sgu, w_pool, pool_scale,
           w_branch, w_out, w_q_mem, w_kv_mem, w_o_mem, w_up, w_down):
    b, s, d = x.shape
    assert d == D_MODEL and s % TOKEN_BLOCK == 0 and TOKEN_BLOCK % ATT_BLOCK == 0 and s % WIDE_BLOCK == 0
    xt = x.reshape(b * s, d)

    k_mem, v_mem = _mem_kv(mem, g_mem.reshape(DEPTH, 1, D_MODEL), w_kv_mem.astype(BF16))
    b_sp_full = jnp.repeat(jnp.swapaxes(b_spatial, 1, 2), SGU_GROUP_DIM, axis=2)
    g_sgu3 = g_sgu.reshape(DEPTH, 1, SGU_DIM)
    pool_scale3 = pool_scale.reshape(DEPTH, 1, POOL_DIM)
    w_in_b, w_pool_b, w_branch_b, w_out_b = (a.astype(BF16) for a in (w_in, w_pool, w_branch, w_out))
    w_q_b, w_o_b, w_up_b, w_down_b = (a.astype(BF16) for a in (w_q_mem, w_o_mem, w_up, w_down))

    for l in range(DEPTH):
        xt = _mixer(l, xt, attn_sinks[l], g_norm, w_in_b, w_spatial, b_sp_full, g_sgu3, w_pool_b,
                    pool_scale3, w_branch_b, w_out_b, s)
        xt = _mem_attn(l, xt, g_norm, w_q_b, k_mem, v_mem, w_o_b, s)
        xt = _ffn(l, xt, g_norm, w_up_b, w_down_b)
    return xt.reshape(b, s, d)
```

```python
import functools
import math

import jax
import jax.numpy as jnp
from jax import lax
from jax.experimental import pallas as pl
from jax.experimental.pallas import tpu as pltpu

D_MODEL = 1024
DEPTH = 4
CHUNK = 64
N_MEM = 256
EPS = 1e-6
NEG_INF = -1e30
N_NORMS = 6

BRANCH_DIM = D_MODEL // 2
N_BRANCH = 3
HEAD_DIM = 64
A_Q_HEADS = BRANCH_DIM // HEAD_DIM
A_KV_HEADS = 2
A_GROUP = A_Q_HEADS // A_KV_HEADS
A_Q_DIM = A_Q_HEADS * HEAD_DIM
A_KV_DIM = A_KV_HEADS * HEAD_DIM
ATT_BLOCK = 128
SGU_CHUNK = 128
SGU_GROUPS = 4
SGU_DIM = BRANCH_DIM
SGU_GROUP_DIM = SGU_DIM // SGU_GROUPS
POOL_WINDOWS = (2, 4, 8, 16)
POOL_GROUPS = 4
POOL_DIM = BRANCH_DIM
POOL_GROUP_DIM = POOL_DIM // POOL_GROUPS
POOL_HALO = 16
IN_DIM = A_Q_DIM + 2 * A_KV_DIM + 2 * SGU_DIM + POOL_DIM + N_BRANCH * D_MODEL
MEM_HEADS = 4
MEM_HEAD_DIM = 128
MEM_DIM = MEM_HEADS * MEM_HEAD_DIM
D_FF = 4 * D_MODEL

OFF_Q = 0
OFF_K = OFF_Q + A_Q_DIM
OFF_V = OFF_K + A_KV_DIM
OFF_SU = OFF_V + A_KV_DIM
OFF_SV = OFF_SU + SGU_DIM
OFF_PC = OFF_SV + SGU_DIM
OFF_GATE = OFF_PC + POOL_DIM

LANES = 128
BF16_SUBLANES = 16
TOKEN_BLOCK = 512
WIDE_BLOCK = 1024
FF_CHUNK = 1024
FF_PARTS = 4
MEM_PARTS = 2
MEM_LAG = 2
VMEM_LIMIT_BYTES = 56 * 1024 * 1024
N_MIXER_IN = 11

BF16 = jnp.bfloat16
F32 = jnp.float32


def _rms(x, g):
    ms = jnp.mean(x * x, axis=-1, keepdims=True)
    return x * lax.rsqrt(ms + EPS) * g


def _dot(a, b):
    return jnp.dot(a, b, preferred_element_type=F32)


def _dot_nt(a, b):
    return lax.dot_general(a, b, (((1,), (1,)), ((), ())), preferred_element_type=F32)


def _layer_spec(stacked, layer):
    tail = (0,) * (stacked.ndim - 1)
    return pl.BlockSpec((None,) + stacked.shape[1:], lambda *_: (layer,) + tail,
                        pipeline_mode=pl.Buffered(1))


def _whole_spec(a):
    zeros = (0,) * a.ndim
    return pl.BlockSpec(a.shape, lambda *_: zeros, pipeline_mode=pl.Buffered(1))


def _cast_plan(stacked_f32, layer, n_steps, step_of=lambda i: i):
    in_specs, out_specs, out_shapes = [], [], []
    for a in stacked_f32:
        r, c = a.shape[1:]
        rb = r // n_steps
        assert rb * n_steps == r and rb % BF16_SUBLANES == 0
        in_specs.append(pl.BlockSpec((None, rb, c), lambda *g: (layer, step_of(*g), 0)))
        out_specs.append(pl.BlockSpec((rb, c), lambda *g: (step_of(*g), 0)))
        out_shapes.append(jax.ShapeDtypeStruct((r, c), BF16))
    return in_specs, out_specs, out_shapes


def _cast_rows(srcs, dsts):
    for src, dst in zip(srcs, dsts):
        dst[...] = src[...].astype(BF16)


def _mixer_kernel(blocks_per_seq, n_cast, *refs):
    (sinks_ref, x_ref, g_ref, w_in_ref, w_sp_ref, b_sp_ref, g_sgu_ref, w_pool_ref, pool_scale_ref,
     w_branch_ref, w_out_ref) = refs[:N_MIXER_IN]
    cast_src = refs[N_MIXER_IN:N_MIXER_IN + n_cast]
    o_ref = refs[N_MIXER_IN + n_cast]
    cast_dst = refs[N_MIXER_IN + n_cast + 1:N_MIXER_IN + 2 * n_cast + 1]
    kbuf, vbuf, cbuf, ya_ref, yb_ref, yc_ref = refs[N_MIXER_IN + 2 * n_cast + 1:]
    _cast_rows(cast_src, cast_dst)
    tb = x_ref.shape[0]
    n_qblk = tb // ATT_BLOCK
    step = pl.program_id(0)
    first = (step % blocks_per_seq) == 0
    seq_pos0 = (step % blocks_per_seq) * tb
    half = tb // 2
    gate_cols = D_MODEL // n_qblk

    @pl.when(first)
    def _():
        kbuf[0:ATT_BLOCK, :] = jnp.zeros((ATT_BLOCK, A_KV_DIM), F32)
        vbuf[0:ATT_BLOCK, :] = jnp.zeros((ATT_BLOCK, A_KV_DIM), F32)
        cbuf[0:POOL_HALO, :] = jnp.zeros((POOL_HALO, POOL_DIM), F32)

    hb = jnp.concatenate(
        [_rms(x_ref[r * half:(r + 1) * half, :], g_ref[0:1, :]).astype(BF16) for r in range(2)], axis=0)

    qkv = _dot(hb, w_in_ref[:, OFF_Q:OFF_SU])
    pc = _dot(hb, w_in_ref[:, OFF_PC:OFF_GATE])

    q = qkv[:, 0:A_Q_DIM] * (1.0 / math.sqrt(HEAD_DIM))
    k_new = qkv[:, A_Q_DIM:A_Q_DIM + A_KV_DIM]
    v_new = qkv[:, A_Q_DIM + A_KV_DIM:A_Q_DIM + 2 * A_KV_DIM]
    kbuf[ATT_BLOCK:ATT_BLOCK + tb, :] = k_new
    vbuf[ATT_BLOCK:ATT_BLOCK + tb, :] = v_new
    k_all = kbuf[...]
    v_all = vbuf[...]
    k_sw = pltpu.roll(k_all, HEAD_DIM, 1)
    v_sw = pltpu.roll(v_all, HEAD_DIM, 1)
    lo_kv = lax.broadcasted_iota(jnp.int32, k_all.shape, 1) < HEAD_DIM
    k_dup = (jnp.where(lo_kv, k_all, k_sw).astype(BF16), jnp.where(lo_kv, k_sw, k_all).astype(BF16))
    one_lo = jnp.where(lo_kv, 1.0, 0.0).astype(BF16)
    one_hi = jnp.where(lo_kv, 0.0, 1.0).astype(BF16)
    v_lo = tuple(jnp.concatenate([jnp.where(lo_kv, t, 0.0).astype(BF16), one_lo], axis=1)
                 for t in (v_all, v_sw))
    v_hi = tuple(jnp.concatenate([jnp.where(lo_kv, 0.0, t).astype(BF16), one_hi], axis=1)
                 for t in (v_sw, v_all))
    kbuf[0:ATT_BLOCK, :] = k_new[tb - ATT_BLOCK:tb, :]
    vbuf[0:ATT_BLOCK, :] = v_new[tb - ATT_BLOCK:tb, :]

    lo_q = lax.broadcasted_iota(jnp.int32, (ATT_BLOCK, LANES), 1) < HEAD_DIM
    q_chunk = lax.broadcasted_iota(jnp.int32, (ATT_BLOCK, 2 * ATT_BLOCK), 0) // CHUNK
    k_col = lax.broadcasted_iota(jnp.int32, (ATT_BLOCK, 2 * ATT_BLOCK), 1)
    k_chunk = k_col // CHUNK
    band = (k_chunk >= q_chunk) & (k_chunk <= q_chunk + 2)
    k_min = jnp.where(first, ATT_BLOCK, 0)

    def scores(j):
        rows = slice(j * ATT_BLOCK, (j + 1) * ATT_BLOCK)
        keys = slice(j * ATT_BLOCK, (j + 2) * ATT_BLOCK)
        out = []
        for h in range(A_KV_HEADS):
            q_stack = []
            for p in (2 * h, 2 * h + 1):
                qp = q[rows, p * LANES:(p + 1) * LANES]
                q_stack.append(jnp.where(lo_q, qp, 0.0).astype(BF16))
                q_stack.append(jnp.where(lo_q, 0.0, qp).astype(BF16))
            out.append(_dot_nt(jnp.concatenate(q_stack, axis=0), k_dup[h][keys]))
        return out

    def attend(j, s_heads):
        rows = slice(j * ATT_BLOCK, (j + 1) * ATT_BLOCK)
        keys = slice(j * ATT_BLOCK, (j + 2) * ATT_BLOCK)
        valid = (band & (k_col >= k_min)) if j == 0 else band
        for h in range(A_KV_HEADS):
            probs, sink_w = [], []
            for a in range(A_GROUP):
                sink = sinks_ref[h * A_GROUP + a]
                s = jnp.where(valid, s_heads[h][a * ATT_BLOCK:(a + 1) * ATT_BLOCK], NEG_INF)
                m = jnp.maximum(jnp.max(s, axis=-1, keepdims=True), sink)
                probs.append(jnp.exp(s - m).astype(BF16))
                sink_w.append(jnp.exp(sink - m))
            for pi in range(2):
                o_aug = _dot(probs[2 * pi], v_lo[h][keys]) + _dot(probs[2 * pi + 1], v_hi[h][keys])
                denom = o_aug[:, LANES:] + jnp.where(lo_q, sink_w[2 * pi], sink_w[2 * pi + 1])
                p = 2 * h + pi
                ya_ref[rows, p * LANES:(p + 1) * LANES] = (o_aug[:, :LANES] / denom).astype(BF16)

    s_cur = scores(0)
    uv = _dot(hb, w_in_ref[:, OFF_SU:OFF_PC])
    gate0 = []
    for j in range(n_qblk):
        s_next = scores(j + 1) if j + 1 < n_qblk else None
        gate0.append(_dot(hb, w_in_ref[:, OFF_GATE + j * gate_cols:OFF_GATE + (j + 1) * gate_cols]))
        attend(j, s_cur)
        s_cur = s_next
    gates = [jnp.concatenate(gate0, axis=1)]

    gates.append(_dot(hb, w_in_ref[:, OFF_GATE + D_MODEL:OFF_GATE + 2 * D_MODEL]))
    u = jax.nn.gelu(uv[:, 0:SGU_DIM])
    vn = _rms(jax.nn.gelu(uv[:, SGU_DIM:2 * SGU_DIM]), g_sgu_ref[...]).astype(BF16)
    sp_row = lax.broadcasted_iota(jnp.int32, (SGU_CHUNK, SGU_CHUNK), 0) // CHUNK
    sp_col = lax.broadcasted_iota(jnp.int32, (SGU_CHUNK, SGU_CHUNK), 1) // CHUNK
    w_sp = [jnp.where(sp_col <= sp_row, w_sp_ref[g], 0.0).astype(BF16) for g in range(SGU_GROUPS)]
    for c in range(tb // SGU_CHUNK):
        rows = slice(c * SGU_CHUNK, (c + 1) * SGU_CHUNK)
        for g in range(SGU_GROUPS):
            cols = slice(g * SGU_GROUP_DIM, (g + 1) * SGU_GROUP_DIM)
            sp = _dot(w_sp[g], vn[rows, cols]) + b_sp_ref[:, cols]
            yb_ref[rows, cols] = (u[rows, cols] * sp).astype(BF16)

    gates.append(_dot(hb, w_in_ref[:, OFF_GATE + 2 * D_MODEL:OFF_GATE + 3 * D_MODEL]))
    cbuf[POOL_HALO:POOL_HALO + tb, :] = pc
    t_pos = seq_pos0 + lax.broadcasted_iota(jnp.int32, (tb, POOL_GROUP_DIM), 0)
    for g, w in enumerate(POOL_WINDOWS):
        cols = slice(g * POOL_GROUP_DIM, (g + 1) * POOL_GROUP_DIM)
        acc = cbuf[:, cols]
        lag = 1
        while lag < w:
            acc = acc + pltpu.roll(acc, lag, 0)
            lag *= 2
        cnt = jnp.minimum(t_pos + 1, w).astype(F32)
        pooled = (acc[POOL_HALO:, :] / cnt - pc[:, cols]).astype(BF16)
        mixed = _dot(pooled, w_pool_ref[cols, :]) * pool_scale_ref[:, cols]
        yc_ref[:, cols] = mixed.astype(BF16)
    cbuf[0:POOL_HALO, :] = pc[tb - POOL_HALO:tb, :]

    projs = [_dot(y_ref[...], w_branch_ref[n * BRANCH_DIM:(n + 1) * BRANCH_DIM, :])
             for n, y_ref in enumerate((ya_ref, yb_ref, yc_ref))]
    for r in range(2):
        rows = slice(r * half, (r + 1) * half)
        merged = None
        for gate, proj in zip(gates, projs):
            term = (0.5 * jnp.tanh(0.5 * gate[rows]) + 0.5) * proj[rows]
            merged = term if merged is None else merged + term
        out = _dot(merged.astype(BF16), w_out_ref[...])
        o_ref[rows, :] = x_ref[rows, :] + _rms(out, g_ref[1:2, :])


def _mixer(layer, x, sinks, g_norm, w_in, w_sp, b_sp_full, g_sgu, w_pool, pool_scale, w_branch, w_out,
           to_cast, seq_len):
    t, d = x.shape
    tb = TOKEN_BLOCK
    assert D_MODEL % (tb // ATT_BLOCK) == 0 and tb % 2 == 0
    n_steps = t // tb
    kern = functools.partial(_mixer_kernel, seq_len // tb, len(to_cast))
    cast_in, cast_out, cast_shapes = _cast_plan(to_cast, layer, n_steps)
    x_spec = pl.BlockSpec((tb, d), lambda i: (i, 0))
    outs = pl.pallas_call(
        kern,
        out_shape=[jax.ShapeDtypeStruct((t, d), F32)] + cast_shapes,
        grid=(n_steps,),
        in_specs=[pl.BlockSpec(memory_space=pltpu.SMEM), x_spec, _layer_spec(g_norm, layer),
                  _whole_spec(w_in), _layer_spec(w_sp, layer), _layer_spec(b_sp_full, layer),
                  _layer_spec(g_sgu, layer), _whole_spec(w_pool), _layer_spec(pool_scale, layer),
                  _whole_spec(w_branch), _whole_spec(w_out)] + cast_in,
        out_specs=[x_spec] + cast_out,
        scratch_shapes=[
            pltpu.VMEM((ATT_BLOCK + tb, A_KV_DIM), F32),
            pltpu.VMEM((ATT_BLOCK + tb, A_KV_DIM), F32),
            pltpu.VMEM((POOL_HALO + tb, POOL_DIM), F32),
            pltpu.VMEM((tb, BRANCH_DIM), BF16),
            pltpu.VMEM((tb, BRANCH_DIM), BF16),
            pltpu.VMEM((tb, BRANCH_DIM), BF16),
        ],
        compiler_params=pltpu.CompilerParams(
            dimension_semantics=("arbitrary",), vmem_limit_bytes=VMEM_LIMIT_BYTES),
        name="mixer",
    )(sinks, x, g_norm, w_in, w_sp, b_sp_full, g_sgu, w_pool, pool_scale, w_branch, w_out, *to_cast)
    return outs[0], outs[1:]


def _mem_kv_kernel(n_cast, mem_ref, g_ref, w_ref, *refs):
    cast_src = refs[:n_cast]
    k_ref, v_ref = refs[n_cast:n_cast + 2]
    cast_dst = refs[n_cast + 2:2 * n_cast + 2]
    w_bf = refs[2 * n_cast + 2]
    _cast_rows(cast_src, cast_dst)

    @pl.when(pl.program_id(1) == 0)
    def _():
        w_bf[...] = w_ref[...].astype(BF16)

    mem_n = _rms(mem_ref[...], g_ref[...]).astype(BF16)
    kv = _dot(mem_n, w_bf[...])
    k_ref[...] = kv[:, 0:MEM_DIM].astype(BF16)
    v_ref[...] = kv[:, MEM_DIM:2 * MEM_DIM].astype(BF16)


def _mem_kv(mem, g_mem, w_kv, to_cast):
    b = mem.shape[0]
    out = jax.ShapeDtypeStruct((DEPTH, b, N_MEM, MEM_DIM), BF16)
    kv_spec = pl.BlockSpec((None, None, N_MEM, MEM_DIM), lambda l, i: (l, i, 0, 0))
    cast_in, cast_out, cast_shapes = _cast_plan(to_cast, 0, DEPTH * b, lambda l, i: l * b + i)
    outs = pl.pallas_call(
        functools.partial(_mem_kv_kernel, len(to_cast)),
        out_shape=[out, out] + cast_shapes,
        grid=(DEPTH, b),
        in_specs=[
            pl.BlockSpec((None, N_MEM, D_MODEL), lambda l, i: (i, 0, 0)),
            pl.BlockSpec((None, 1, D_MODEL), lambda l, i: (l, 0, 0)),
            pl.BlockSpec((None, D_MODEL, 2 * MEM_DIM), lambda l, i: (l, 0, 0)),
        ] + cast_in,
        out_specs=[kv_spec, kv_spec] + cast_out,
        scratch_shapes=[pltpu.VMEM((D_MODEL, 2 * MEM_DIM), BF16)],
        compiler_params=pltpu.CompilerParams(
            dimension_semantics=("arbitrary", "arbitrary"), vmem_limit_bytes=VMEM_LIMIT_BYTES),
        name="mem_kv",
    )(mem, g_mem, w_kv, *to_cast)
    return outs[0], outs[1], outs[2:]


def _mem_attn_kernel(x_ref, g_ref, w_q_ref, k_ref, v_ref, w_o_ref, o_ref, om_ref):
    rows_per_part = x_ref.shape[0] // MEM_PARTS
    parts = [slice(p * rows_per_part, (p + 1) * rows_per_part) for p in range(MEM_PARTS)]

    def query(rows):
        hb = _rms(x_ref[rows, :], g_ref[2:3, :]).astype(BF16)
        return (_dot(hb, w_q_ref[...]) * (1.0 / math.sqrt(MEM_HEAD_DIM))).astype(BF16)

    def scores(q):
        return [_dot_nt(q[:, h * MEM_HEAD_DIM:(h + 1) * MEM_HEAD_DIM],
                        k_ref[:, h * MEM_HEAD_DIM:(h + 1) * MEM_HEAD_DIM]) for h in range(MEM_HEADS)]

    ones = jnp.ones((N_MEM, MEM_HEAD_DIM), BF16)
    v_aug = [jnp.concatenate([v_ref[:, h * MEM_HEAD_DIM:(h + 1) * MEM_HEAD_DIM], ones], axis=1)
             for h in range(MEM_HEADS)]

    def attend(rows, s_heads):
        for h, s in enumerate(s_heads):
            cols = slice(h * MEM_HEAD_DIM, (h + 1) * MEM_HEAD_DIM)
            m = jnp.max(s, axis=-1, keepdims=True)
            o_aug = _dot(jnp.exp(s - m).astype(BF16), v_aug[h])
            om_ref[rows, cols] = (o_aug[:, :MEM_HEAD_DIM] / o_aug[:, MEM_HEAD_DIM:]).astype(BF16)
        ym = _dot(om_ref[rows, :], w_o_ref[...])
        o_ref[rows, :] = x_ref[rows, :] + _rms(ym, g_ref[3:4, :])

    pending = []
    for t in range(MEM_PARTS + MEM_LAG):
        if t < MEM_PARTS:
            pending.append(scores(query(parts[t])))
        if t >= MEM_LAG:
            attend(parts[t - MEM_LAG], pending[t - MEM_LAG])


def _mem_attn(layer, x, g_norm, w_q, k_mem, v_mem, w_o, seq_len):
    t, d = x.shape
    tb = WIDE_BLOCK
    bps = seq_len // tb
    kv_spec = pl.BlockSpec((None, None, N_MEM, MEM_DIM), lambda i: (layer, i // bps, 0, 0))
    return pl.pallas_call(
        _mem_attn_kernel,
        out_shape=jax.ShapeDtypeStruct((t, d), F32),
        grid=(t // tb,),
        in_specs=[
            pl.BlockSpec((tb, d), lambda i: (i, 0)),
            _layer_spec(g_norm, layer),
            _whole_spec(w_q),
            kv_spec,
            kv_spec,
            _whole_spec(w_o),
        ],
        out_specs=pl.BlockSpec((tb, d), lambda i: (i, 0)),
        scratch_shapes=[pltpu.VMEM((tb, MEM_DIM), BF16)],
        compiler_params=pltpu.CompilerParams(
            dimension_semantics=("arbitrary",), vmem_limit_bytes=VMEM_LIMIT_BYTES),
        name="mem_attn",
    )(x, g_norm, w_q, k_mem, v_mem, w_o)


def _ffn_kernel(n_cast, x_ref, g_ref, w_up_ref, w_down_ref, *refs):
    cast_src = refs[:n_cast]
    o_ref = refs[n_cast]
    cast_dst = refs[n_cast + 1:]
    _cast_rows(cast_src, cast_dst)
    rows_per_part = x_ref.shape[0] // FF_PARTS
    parts = [slice(p * rows_per_part, (p + 1) * rows_per_part) for p in range(FF_PARTS)]
    hbs, yfs = [], [None] * FF_PARTS
    for c in range(D_FF // FF_CHUNK):
        cols = slice(c * FF_CHUNK, (c + 1) * FF_CHUNK)
        for p, rows in enumerate(parts):
            if c == 0:
                hbs.append(_rms(x_ref[rows, :], g_ref[4:5, :]).astype(BF16))
            up = jnp.maximum(_dot(hbs[p], w_up_ref[:, cols]), 0.0)
            part = _dot((up * up).astype(BF16), w_down_ref[cols, :])
            yfs[p] = part if yfs[p] is None else yfs[p] + part
    for p, rows in enumerate(parts):
        o_ref[rows, :] = x_ref[rows, :] + _rms(yfs[p], g_ref[5:6, :])


def _ffn(layer, x, g_norm, w_up, w_down, to_cast, cast_layer):
    t, d = x.shape
    tb = WIDE_BLOCK
    n_steps = t // tb
    cast_in, cast_out, cast_shapes = _cast_plan(to_cast, cast_layer, n_steps)
    x_spec = pl.BlockSpec((tb, d), lambda i: (i, 0))
    outs = pl.pallas_call(
        functools.partial(_ffn_kernel, len(to_cast)),
        out_shape=[jax.ShapeDtypeStruct((t, d), F32)] + cast_shapes,
        grid=(n_steps,),
        in_specs=[x_spec, _layer_spec(g_norm, layer), _whole_spec(w_up), _whole_spec(w_down)] + cast_in,
        out_specs=[x_spec] + cast_out,
        compiler_params=pltpu.CompilerParams(
            dimension_semantics=("arbitrary",), vmem_limit_bytes=VMEM_LIMIT_BYTES),
        name="ffn",
    )(x, g_norm, w_up, w_down, *to_cast)
    return outs[0], outs[1:]


def kernel(x, mem, g_norm, g_mem, w_in, attn_sinks, w_spatial, b_spatial, g_sgu, w_pool, pool_scale,
           w_branch, w_out, w_q_mem, w_kv_mem, w_o_mem, w_up, w_down):
    b, s, d = x.shape
    assert d == D_MODEL and s % TOKEN_BLOCK == 0 and TOKEN_BLOCK % ATT_BLOCK == 0 and s % WIDE_BLOCK == 0
    xt = x.reshape(b * s, d)

    b_sp_full = jnp.repeat(jnp.swapaxes(b_spatial, 1, 2), SGU_GROUP_DIM, axis=2)
    g_sgu3 = g_sgu.reshape(DEPTH, 1, SGU_DIM)
    pool_scale3 = pool_scale.reshape(DEPTH, 1, POOL_DIM)
    mixer_w = (w_in, w_pool.reshape(DEPTH, POOL_DIM, POOL_GROUP_DIM),
               w_branch.reshape(DEPTH, N_BRANCH * BRANCH_DIM, D_MODEL), w_out)
    later_w = (w_q_mem, w_o_mem, w_up, w_down)

    k_mem, v_mem, mixer_b = _mem_kv(mem, g_mem.reshape(DEPTH, 1, D_MODEL), w_kv_mem, mixer_w)
    for l in range(DEPTH):
        w_in_b, w_pool_b, w_branch_b, w_out_b = mixer_b
        xt, (w_q_b, w_o_b, w_up_b, w_down_b) = _mixer(
            l, xt, attn_sinks[l], g_norm, w_in_b, w_spatial, b_sp_full, g_sgu3, w_pool_b, pool_scale3,
            w_branch_b, w_out_b, later_w, s)
        xt = _mem_attn(l, xt, g_norm, w_q_b, k_mem, v_mem, w_o_b, s)
        xt, mixer_b = _ffn(l, xt, g_norm, w_up_b, w_down_b, mixer_w if l + 1 < DEPTH else (), l + 1)
    return xt.reshape(b, s, d)
```

```python
import functools
import math

import jax
import jax.numpy as jnp
from jax import lax
from jax.experimental import pallas as pl
from jax.experimental.pallas import tpu as pltpu

D_MODEL = 1024
DEPTH = 4
CHUNK = 64
N_MEM = 256
EPS = 1e-6
NEG_INF = -1e30
LOG2_E = math.log2(math.e)
N_NORMS = 6

BRANCH_DIM = D_MODEL // 2
N_BRANCH = 3
HEAD_DIM = 64
A_Q_HEADS = BRANCH_DIM // HEAD_DIM
A_KV_HEADS = 2
A_GROUP = A_Q_HEADS // A_KV_HEADS
A_Q_DIM = A_Q_HEADS * HEAD_DIM
A_KV_DIM = A_KV_HEADS * HEAD_DIM
ATT_BLOCK = 128
SGU_CHUNK = 128
SGU_GROUPS = 4
SGU_DIM = BRANCH_DIM
SGU_GROUP_DIM = SGU_DIM // SGU_GROUPS
POOL_WINDOWS = (2, 4, 8, 16)
POOL_GROUPS = 4
POOL_DIM = BRANCH_DIM
POOL_GROUP_DIM = POOL_DIM // POOL_GROUPS
POOL_HALO = 16
IN_DIM = A_Q_DIM + 2 * A_KV_DIM + 2 * SGU_DIM + POOL_DIM + N_BRANCH * D_MODEL
MEM_HEADS = 4
MEM_HEAD_DIM = 128
MEM_DIM = MEM_HEADS * MEM_HEAD_DIM
D_FF = 4 * D_MODEL

OFF_Q = 0
OFF_K = OFF_Q + A_Q_DIM
OFF_V = OFF_K + A_KV_DIM
OFF_SU = OFF_V + A_KV_DIM
OFF_SV = OFF_SU + SGU_DIM
OFF_PC = OFF_SV + SGU_DIM
OFF_GATE = OFF_PC + POOL_DIM

LANES = 128
BF16_SUBLANES = 16
TOKEN_BLOCK = 512
OUT_PARTS = (256, 256)
GATE_COLS = 256
WIDE_BLOCK = 1024
FF_CHUNK = 1024
FF_PARTS = 4
MEM_PARTS = 2
MEM_LAG = 2
VMEM_LIMIT_BYTES = 56 * 1024 * 1024
N_MIXER_IN = 11

BF16 = jnp.bfloat16
F32 = jnp.float32


def _rms(x, g):
    ms = jnp.mean(x * x, axis=-1, keepdims=True)
    return x * lax.rsqrt(ms + EPS) * g


def _gelu(x):
    c = math.sqrt(2.0 / math.pi)
    t = jnp.tanh(x * (c + (c * 0.044715) * (x * x)))
    hx = 0.5 * x
    return hx + hx * t


def _sigmoid(x):
    return 0.5 * jnp.tanh(0.5 * x) + 0.5


def _dot(a, b):
    return jnp.dot(a, b, preferred_element_type=F32)


def _dot_nt(a, b):
    return lax.dot_general(a, b, (((1,), (1,)), ((), ())), preferred_element_type=F32)


def _layer_spec(stacked, layer):
    tail = (0,) * (stacked.ndim - 1)
    return pl.BlockSpec((None,) + stacked.shape[1:], lambda *_: (layer,) + tail,
                        pipeline_mode=pl.Buffered(1))


def _whole_spec(a):
    zeros = (0,) * a.ndim
    return pl.BlockSpec(a.shape, lambda *_: zeros, pipeline_mode=pl.Buffered(1))


def _cast_plan(stacked_f32, layer, n_steps, step_of=lambda i: i):
    in_specs, out_specs, out_shapes = [], [], []
    for a in stacked_f32:
        r, c = a.shape[1:]
        rb = r // n_steps
        assert rb * n_steps == r and rb % BF16_SUBLANES == 0
        in_specs.append(pl.BlockSpec((None, rb, c), lambda *g: (layer, step_of(*g), 0)))
        out_specs.append(pl.BlockSpec((rb, c), lambda *g: (step_of(*g), 0)))
        out_shapes.append(jax.ShapeDtypeStruct((r, c), BF16))
    return in_specs, out_specs, out_shapes


def _cast_rows(srcs, dsts):
    for src, dst in zip(srcs, dsts):
        dst[...] = src[...].astype(BF16)


def _mixer_kernel(blocks_per_seq, n_cast, *refs):
    (sinks_ref, x_ref, g_ref, w_in_ref, w_sp_ref, b_sp_ref, g_sgu_ref, w_pool_ref, pool_scale_ref,
     w_branch_ref, w_out_ref) = refs[:N_MIXER_IN]
    cast_src = refs[N_MIXER_IN:N_MIXER_IN + n_cast]
    o_ref = refs[N_MIXER_IN + n_cast]
    cast_dst = refs[N_MIXER_IN + n_cast + 1:N_MIXER_IN + 2 * n_cast + 1]
    kbuf, vbuf, cbuf, ya_ref, yb_ref, yc_ref = refs[N_MIXER_IN + 2 * n_cast + 1:]
    tb = x_ref.shape[0]
    n_qblk = tb // ATT_BLOCK
    step = pl.program_id(0)
    first = (step % blocks_per_seq) == 0
    seq_pos0 = (step % blocks_per_seq) * tb
    half = tb // 2
    blocks_per_gate = n_qblk // (D_MODEL // GATE_COLS)

    @pl.when(first)
    def _():
        kbuf[0:ATT_BLOCK, :] = jnp.zeros((ATT_BLOCK, A_KV_DIM), F32)
        vbuf[0:ATT_BLOCK, :] = jnp.zeros((ATT_BLOCK, A_KV_DIM), F32)
        cbuf[0:POOL_HALO, :] = jnp.zeros((POOL_HALO, POOL_DIM), F32)

    hb_halves = [_rms(x_ref[r * half:(r + 1) * half, :], g_ref[0:1, :]).astype(BF16) for r in range(2)]
    qkv = jnp.concatenate([_dot(hb_r, w_in_ref[:, OFF_Q:OFF_SU]) for hb_r in hb_halves], axis=0)
    hb = jnp.concatenate(hb_halves, axis=0)
    pc = _dot(hb, w_in_ref[:, OFF_PC:OFF_GATE])

    q = qkv[:, 0:A_Q_DIM] * (LOG2_E / math.sqrt(HEAD_DIM))
    k_new = qkv[:, A_Q_DIM:A_Q_DIM + A_KV_DIM]
    v_new = qkv[:, A_Q_DIM + A_KV_DIM:A_Q_DIM + 2 * A_KV_DIM]
    kbuf[ATT_BLOCK:ATT_BLOCK + tb, :] = k_new
    vbuf[ATT_BLOCK:ATT_BLOCK + tb, :] = v_new
    k_all = kbuf[...]
    v_all = vbuf[...]
    k_sw = pltpu.roll(k_all, HEAD_DIM, 1)
    v_sw = pltpu.roll(v_all, HEAD_DIM, 1)
    lo_kv = lax.broadcasted_iota(jnp.int32, k_all.shape, 1) < HEAD_DIM
    k_dup = (jnp.where(lo_kv, k_all, k_sw).astype(BF16), jnp.where(lo_kv, k_sw, k_all).astype(BF16))
    one_lo = jnp.where(lo_kv, 1.0, 0.0).astype(BF16)
    one_hi = jnp.where(lo_kv, 0.0, 1.0).astype(BF16)
    v_lo = tuple(jnp.concatenate([jnp.where(lo_kv, t, 0.0).astype(BF16), one_lo], axis=1)
                 for t in (v_all, v_sw))
    v_hi = tuple(jnp.concatenate([jnp.where(lo_kv, 0.0, t).astype(BF16), one_hi], axis=1)
                 for t in (v_sw, v_all))
    kbuf[0:ATT_BLOCK, :] = k_new[tb - ATT_BLOCK:tb, :]
    vbuf[0:ATT_BLOCK, :] = v_new[tb - ATT_BLOCK:tb, :]

    lo_q = lax.broadcasted_iota(jnp.int32, (ATT_BLOCK, LANES), 1) < HEAD_DIM
    q_chunk = lax.broadcasted_iota(jnp.int32, (ATT_BLOCK, 2 * ATT_BLOCK), 0) // CHUNK
    k_col = lax.broadcasted_iota(jnp.int32, (ATT_BLOCK, 2 * ATT_BLOCK), 1)
    k_chunk = k_col // CHUNK
    band = (k_chunk >= q_chunk) & (k_chunk <= q_chunk + 2)
    k_min = jnp.where(first, ATT_BLOCK, 0)

    def scores(j):
        rows = slice(j * ATT_BLOCK, (j + 1) * ATT_BLOCK)
        keys = slice(j * ATT_BLOCK, (j + 2) * ATT_BLOCK)
        out = []
        for h in range(A_KV_HEADS):
            q_stack = []
            for p in (2 * h, 2 * h + 1):
                qp = q[rows, p * LANES:(p + 1) * LANES]
                q_stack.append(jnp.where(lo_q, qp, 0.0).astype(BF16))
                q_stack.append(jnp.where(lo_q, 0.0, qp).astype(BF16))
            out.append(_dot_nt(jnp.concatenate(q_stack, axis=0), k_dup[h][keys]))
        return out

    def attend(j, s_heads):
        rows = slice(j * ATT_BLOCK, (j + 1) * ATT_BLOCK)
        keys = slice(j * ATT_BLOCK, (j + 2) * ATT_BLOCK)
        valid = (band & (k_col >= k_min)) if j == 0 else band
        for h in range(A_KV_HEADS):
            probs, sink_w = [], []
            for a in range(A_GROUP):
                sink = sinks_ref[h * A_GROUP + a] * LOG2_E
                s = jnp.where(valid, s_heads[h][a * ATT_BLOCK:(a + 1) * ATT_BLOCK], NEG_INF)
                m = jnp.maximum(jnp.max(s, axis=-1, keepdims=True), sink)
                probs.append(jnp.exp2(s - m).astype(BF16))
                sink_w.append(jnp.exp2(sink - m))
            for pi in range(2):
                o_aug = _dot(probs[2 * pi], v_lo[h][keys]) + _dot(probs[2 * pi + 1], v_hi[h][keys])
                denom = o_aug[:, LANES:] + jnp.where(lo_q, sink_w[2 * pi], sink_w[2 * pi + 1])
                p = 2 * h + pi
                ya_ref[rows, p * LANES:(p + 1) * LANES] = (o_aug[:, :LANES] / denom).astype(BF16)

    s_cur = scores(0)
    uv = _dot(hb, w_in_ref[:, OFF_SU:OFF_PC])
    gate0 = []
    for j in range(n_qblk):
        s_next = scores(j + 1) if j + 1 < n_qblk else None
        if j % blocks_per_gate == 0:
            c0 = OFF_GATE + (j // blocks_per_gate) * GATE_COLS
            gate0.append(_dot(hb, w_in_ref[:, c0:c0 + GATE_COLS]))
        attend(j, s_cur)
        s_cur = s_next
    gates = [_sigmoid(jnp.concatenate(gate0, axis=1))]

    _cast_rows(cast_src, cast_dst)

    gates.append(_sigmoid(_dot(hb, w_in_ref[:, OFF_GATE + D_MODEL:OFF_GATE + 2 * D_MODEL])))
    u = _gelu(uv[:, 0:SGU_DIM])
    vn = _rms(_gelu(uv[:, SGU_DIM:2 * SGU_DIM]), g_sgu_ref[...]).astype(BF16)
    sp_row = lax.broadcasted_iota(jnp.int32, (SGU_CHUNK, SGU_CHUNK), 0) // CHUNK
    sp_col = lax.broadcasted_iota(jnp.int32, (SGU_CHUNK, SGU_CHUNK), 1) // CHUNK
    w_sp = [jnp.where(sp_col <= sp_row, w_sp_ref[g], 0.0).astype(BF16) for g in range(SGU_GROUPS)]
    for c in range(tb // SGU_CHUNK):
        rows = slice(c * SGU_CHUNK, (c + 1) * SGU_CHUNK)
        for g in range(SGU_GROUPS):
            cols = slice(g * SGU_GROUP_DIM, (g + 1) * SGU_GROUP_DIM)
            sp = _dot(w_sp[g], vn[rows, cols]) + b_sp_ref[:, cols]
            yb_ref[rows, cols] = (u[rows, cols] * sp).astype(BF16)

    gates.append(_sigmoid(_dot(hb, w_in_ref[:, OFF_GATE + 2 * D_MODEL:OFF_GATE + 3 * D_MODEL])))
    cbuf[POOL_HALO:POOL_HALO + tb, :] = pc
    t_pos = seq_pos0 + lax.broadcasted_iota(jnp.int32, (tb, POOL_GROUP_DIM), 0)
    for g, w in enumerate(POOL_WINDOWS):
        cols = slice(g * POOL_GROUP_DIM, (g + 1) * POOL_GROUP_DIM)
        acc = cbuf[:, cols]
        lag = 1
        while lag < w:
            acc = acc + pltpu.roll(acc, lag, 0)
            lag *= 2
        cnt = jnp.minimum(t_pos + 1, w).astype(F32)
        pooled = (acc[POOL_HALO:, :] / cnt - pc[:, cols]).astype(BF16)
        mixed = _dot(pooled, w_pool_ref[cols, :]) * pool_scale_ref[:, cols]
        yc_ref[:, cols] = mixed.astype(BF16)
    cbuf[0:POOL_HALO, :] = pc[tb - POOL_HALO:tb, :]

    merged = []
    for r in range(2):
        rows = slice(r * half, (r + 1) * half)
        acc = None
        for n, y_ref in enumerate((ya_ref, yb_ref, yc_ref)):
            proj = _dot(y_ref[rows, :], w_branch_ref[n * BRANCH_DIM:(n + 1) * BRANCH_DIM, :])
            term = gates[n][rows] * proj
            acc = term if acc is None else acc + term
        merged.append(acc.astype(BF16))
    merged = jnp.concatenate(merged, axis=0)
    start = 0
    for size in OUT_PARTS:
        rows = slice(start, start + size)
        out = _dot(merged[rows], w_out_ref[...])
        o_ref[rows, :] = x_ref[rows, :] + _rms(out, g_ref[1:2, :])
        start += size


def _mixer(layer, x, sinks, g_norm, w_in, w_sp, b_sp_full, g_sgu, w_pool, pool_scale, w_branch, w_out,
           to_cast, seq_len):
    t, d = x.shape
    tb = TOKEN_BLOCK
    assert (tb // ATT_BLOCK) % (D_MODEL // GATE_COLS) == 0 and tb % 2 == 0 and sum(OUT_PARTS) == tb
    n_steps = t // tb
    kern = functools.partial(_mixer_kernel, seq_len // tb, len(to_cast))
    cast_in, cast_out, cast_shapes = _cast_plan(to_cast, layer, n_steps)
    x_spec = pl.BlockSpec((tb, d), lambda i: (i, 0))
    outs = pl.pallas_call(
        kern,
        out_shape=[jax.ShapeDtypeStruct((t, d), F32)] + cast_shapes,
        grid=(n_steps,),
        in_specs=[pl.BlockSpec(memory_space=pltpu.SMEM), x_spec, _layer_spec(g_norm, layer),
                  _whole_spec(w_in), _layer_spec(w_sp, layer), _layer_spec(b_sp_full, layer),
                  _layer_spec(g_sgu, layer), _whole_spec(w_pool), _layer_spec(pool_scale, layer),
                  _whole_spec(w_branch), _whole_spec(w_out)] + cast_in,
        out_specs=[x_spec] + cast_out,
        scratch_shapes=[
            pltpu.VMEM((ATT_BLOCK + tb, A_KV_DIM), F32),
            pltpu.VMEM((ATT_BLOCK + tb, A_KV_DIM), F32),
            pltpu.VMEM((POOL_HALO + tb, POOL_DIM), F32),
            pltpu.VMEM((tb, BRANCH_DIM), BF16),
            pltpu.VMEM((tb, BRANCH_DIM), BF16),
            pltpu.VMEM((tb, BRANCH_DIM), BF16),
        ],
        compiler_params=pltpu.CompilerParams(
            dimension_semantics=("arbitrary",), vmem_limit_bytes=VMEM_LIMIT_BYTES),
        name="mixer",
    )(sinks, x, g_norm, w_in, w_sp, b_sp_full, g_sgu, w_pool, pool_scale, w_branch, w_out, *to_cast)
    return outs[0], outs[1:]


def _mem_kv_kernel(n_cast, mem_ref, g_ref, w_ref, *refs):
    cast_src = refs[:n_cast]
    k_ref, v_ref = refs[n_cast:n_cast + 2]
    cast_dst = refs[n_cast + 2:2 * n_cast + 2]
    w_bf = refs[2 * n_cast + 2]
    _cast_rows(cast_src, cast_dst)

    @pl.when(pl.program_id(1) == 0)
    def _():
        w_bf[...] = w_ref[...].astype(BF16)

    mem_n = _rms(mem_ref[...], g_ref[...]).astype(BF16)
    kv = _dot(mem_n, w_bf[...])
    k_ref[...] = kv[:, 0:MEM_DIM].astype(BF16)
    v_ref[...] = kv[:, MEM_DIM:2 * MEM_DIM].astype(BF16)


def _mem_kv(mem, g_mem, w_kv, to_cast):
    b = mem.shape[0]
    out = jax.ShapeDtypeStruct((DEPTH, b, N_MEM, MEM_DIM), BF16)
    kv_spec = pl.BlockSpec((None, None, N_MEM, MEM_DIM), lambda l, i: (l, i, 0, 0))
    cast_in, cast_out, cast_shapes = _cast_plan(to_cast, 0, DEPTH * b, lambda l, i: l * b + i)
    outs = pl.pallas_call(
        functools.partial(_mem_kv_kernel, len(to_cast)),
        out_shape=[out, out] + cast_shapes,
        grid=(DEPTH, b),
        in_specs=[
            pl.BlockSpec((None, N_MEM, D_MODEL), lambda l, i: (i, 0, 0)),
            pl.BlockSpec((None, 1, D_MODEL), lambda l, i: (l, 0, 0)),
            pl.BlockSpec((None, D_MODEL, 2 * MEM_DIM), lambda l, i: (l, 0, 0)),
        ] + cast_in,
        out_specs=[kv_spec, kv_spec] + cast_out,
        scratch_shapes=[pltpu.VMEM((D_MODEL, 2 * MEM_DIM), BF16)],
        compiler_params=pltpu.CompilerParams(
            dimension_semantics=("arbitrary", "arbitrary"), vmem_limit_bytes=VMEM_LIMIT_BYTES),
        name="mem_kv",
    )(mem, g_mem, w_kv, *to_cast)
    return outs[0], outs[1], outs[2:]


def _mem_attn_kernel(x_ref, g_ref, w_q_ref, k_ref, v_ref, w_o_ref, o_ref, om_ref):
    rows_per_part = x_ref.shape[0] // MEM_PARTS
    parts = [slice(p * rows_per_part, (p + 1) * rows_per_part) for p in range(MEM_PARTS)]

    def query(rows):
        hb = _rms(x_ref[rows, :], g_ref[2:3, :]).astype(BF16)
        return (_dot(hb, w_q_ref[...]) * (LOG2_E / math.sqrt(MEM_HEAD_DIM))).astype(BF16)

    def scores(q):
        return [_dot_nt(q[:, h * MEM_HEAD_DIM:(h + 1) * MEM_HEAD_DIM],
                        k_ref[:, h * MEM_HEAD_DIM:(h + 1) * MEM_HEAD_DIM]) for h in range(MEM_HEADS)]

    ones = jnp.ones((N_MEM, MEM_HEAD_DIM), BF16)
    v_aug = [jnp.concatenate([v_ref[:, h * MEM_HEAD_DIM:(h + 1) * MEM_HEAD_DIM], ones], axis=1)
             for h in range(MEM_HEADS)]

    def attend(rows, s_heads):
        for h, s in enumerate(s_heads):
            cols = slice(h * MEM_HEAD_DIM, (h + 1) * MEM_HEAD_DIM)
            m = jnp.max(s, axis=-1, keepdims=True)
            o_aug = _dot(jnp.exp2(s - m).astype(BF16), v_aug[h])
            om_ref[rows, cols] = (o_aug[:, :MEM_HEAD_DIM] / o_aug[:, MEM_HEAD_DIM:]).astype(BF16)
        ym = _dot(om_ref[rows, :], w_o_ref[...])
        o_ref[rows, :] = x_ref[rows, :] + _rms(ym, g_ref[3:4, :])

    pending = []
    for t in range(MEM_PARTS + MEM_LAG):
        if t < MEM_PARTS:
            pending.append(scores(query(parts[t])))
        if t >= MEM_LAG:
            attend(parts[t - MEM_LAG], pending[t - MEM_LAG])


def _mem_attn(layer, x, g_norm, w_q, k_mem, v_mem, w_o, seq_len):
    t, d = x.shape
    tb = WIDE_BLOCK
    bps = seq_len // tb
    kv_spec = pl.BlockSpec((None, None, N_MEM, MEM_DIM), lambda i: (layer, i // bps, 0, 0))
    return pl.pallas_call(
        _mem_attn_kernel,
        out_shape=jax.ShapeDtypeStruct((t, d), F32),
        grid=(t // tb,),
        in_specs=[
            pl.BlockSpec((tb, d), lambda i: (i, 0)),
            _layer_spec(g_norm, layer),
            _whole_spec(w_q),
            kv_spec,
            kv_spec,
            _whole_spec(w_o),
        ],
        out_specs=pl.BlockSpec((tb, d), lambda i: (i, 0)),
        scratch_shapes=[pltpu.VMEM((tb, MEM_DIM), BF16)],
        compiler_params=pltpu.CompilerParams(
            dimension_semantics=("arbitrary",), vmem_limit_bytes=VMEM_LIMIT_BYTES),
        name="mem_attn",
    )(x, g_norm, w_q, k_mem, v_mem, w_o)


def _ffn_kernel(n_cast, x_ref, g_ref, w_up_ref, w_down_ref, *refs):
    cast_src = refs[:n_cast]
    o_ref = refs[n_cast]
    cast_dst = refs[n_cast + 1:]
    rows_per_part = x_ref.shape[0] // FF_PARTS
    parts = [slice(p * rows_per_part, (p + 1) * rows_per_part) for p in range(FF_PARTS)]
    hbs, yfs = [], [None] * FF_PARTS
    for c in range(D_FF // FF_CHUNK):
        cols = slice(c * FF_CHUNK, (c + 1) * FF_CHUNK)
        if c == 1:
            _cast_rows(cast_src, cast_dst)
        for p, rows in enumerate(parts):
            if c == 0:
                hbs.append(_rms(x_ref[rows, :], g_ref[4:5, :]).astype(BF16))
            up = jnp.maximum(_dot(hbs[p], w_up_ref[:, cols]), 0.0)
            part = _dot((up * up).astype(BF16), w_down_ref[cols, :])
            yfs[p] = part if yfs[p] is None else yfs[p] + part
    for p, rows in enumerate(parts):
        o_ref[rows, :] = x_ref[rows, :] + _rms(yfs[p], g_ref[5:6, :])


def _ffn(layer, x, g_norm, w_up, w_down, to_cast, cast_layer):
    t, d = x.shape
    tb = WIDE_BLOCK
    n_steps = t // tb
    cast_in, cast_out, cast_shapes = _cast_plan(to_cast, cast_layer, n_steps)
    x_spec = pl.BlockSpec((tb, d), lambda i: (i, 0))
    outs = pl.pallas_call(
        functools.partial(_ffn_kernel, len(to_cast)),
        out_shape=[jax.ShapeDtypeStruct((t, d), F32)] + cast_shapes,
        grid=(n_steps,),
        in_specs=[x_spec, _layer_spec(g_norm, layer), _whole_spec(w_up), _whole_spec(w_down)] + cast_in,
        out_specs=[x_spec] + cast_out,
        compiler_params=pltpu.CompilerParams(
            dimension_semantics=("arbitrary",), vmem_limit_bytes=VMEM_LIMIT_BYTES),
        name="ffn",
    )(x, g_norm, w_up, w_down, *to_cast)
    return outs[0], outs[1:]


def kernel(x, mem, g_norm, g_mem, w_in, attn_sinks, w_spatial, b_spatial, g_sgu, w_pool, pool_scale,
           w_branch, w_out, w_q_mem, w_kv_mem, w_o_mem, w_up, w_down):
    b, s, d = x.shape
    assert d == D_MODEL and s % TOKEN_BLOCK == 0 and TOKEN_BLOCK % ATT_BLOCK == 0 and s % WIDE_BLOCK == 0
    xt = x.reshape(b * s, d)

    b_sp_full = jnp.repeat(jnp.swapaxes(b_spatial, 1, 2), SGU_GROUP_DIM, axis=2)
    g_sgu3 = g_sgu.reshape(DEPTH, 1, SGU_DIM)
    pool_scale3 = pool_scale.reshape(DEPTH, 1, POOL_DIM)
    mixer_w = (w_in, w_pool.reshape(DEPTH, POOL_DIM, POOL_GROUP_DIM),
               w_branch.reshape(DEPTH, N_BRANCH * BRANCH_DIM, D_MODEL), w_out)
    later_w = (w_q_mem, w_o_mem, w_up, w_down)

    k_mem, v_mem, mixer_b = _mem_kv(mem, g_mem.reshape(DEPTH, 1, D_MODEL), w_kv_mem, mixer_w)
    for l in range(DEPTH):
        w_in_b, w_pool_b, w_branch_b, w_out_b = mixer_b
        xt, (w_q_b, w_o_b, w_up_b, w_down_b) = _mixer(
            l, xt, attn_sinks[l], g_norm, w_in_b, w_spatial, b_sp_full, g_sgu3, w_pool_b, pool_scale3,
            w_branch_b, w_out_b, later_w, s)
        xt = _mem_attn(l, xt, g_norm, w_q_b, k_mem, v_mem, w_o_b, s)
        xt, mixer_b = _ffn(l, xt, g_norm, w_up_b, w_down_b, mixer_w if l + 1 < DEPTH else (), l + 1)
    return xt.reshape(b, s, d)
```

```python
import functools
import math

import jax
import jax.numpy as jnp
from jax import lax
from jax.experimental import pallas as pl
from jax.experimental.pallas import tpu as pltpu

D_MODEL = 1024
DEPTH = 4
CHUNK = 64
N_MEM = 256
EPS = 1e-6
NEG_INF = -1e30
LOG2_E = math.log2(math.e)
N_NORMS = 6

BRANCH_DIM = D_MODEL // 2
N_BRANCH = 3
HEAD_DIM = 64
A_Q_HEADS = BRANCH_DIM // HEAD_DIM
A_KV_HEADS = 2
A_GROUP = A_Q_HEADS // A_KV_HEADS
A_Q_DIM = A_Q_HEADS * HEAD_DIM
A_KV_DIM = A_KV_HEADS * HEAD_DIM
ATT_BLOCK = 128
SGU_CHUNK = 128
SGU_GROUPS = 4
SGU_DIM = BRANCH_DIM
SGU_GROUP_DIM = SGU_DIM // SGU_GROUPS
POOL_WINDOWS = (2, 4, 8, 16)
POOL_GROUPS = 4
POOL_DIM = BRANCH_DIM
POOL_GROUP_DIM = POOL_DIM // POOL_GROUPS
POOL_HALO = 16
IN_DIM = A_Q_DIM + 2 * A_KV_DIM + 2 * SGU_DIM + POOL_DIM + N_BRANCH * D_MODEL
MEM_HEADS = 4
MEM_HEAD_DIM = 128
MEM_DIM = MEM_HEADS * MEM_HEAD_DIM
D_FF = 4 * D_MODEL

OFF_Q = 0
OFF_K = OFF_Q + A_Q_DIM
OFF_V = OFF_K + A_KV_DIM
OFF_SU = OFF_V + A_KV_DIM
OFF_SV = OFF_SU + SGU_DIM
OFF_PC = OFF_SV + SGU_DIM
OFF_GATE = OFF_PC + POOL_DIM

LANES = 128
BF16_SUBLANES = 16
TOKEN_BLOCK = 512
OUT_PARTS = (256, 256)
GATE_COLS = 256
WIDE_BLOCK = 1024
FF_CHUNK = 1024
FF_PARTS = 4
MEM_PARTS = 2
MEM_LAG = 2
VMEM_LIMIT_BYTES = 56 * 1024 * 1024
N_MIXER_IN = 11

BF16 = jnp.bfloat16
F32 = jnp.float32


def _rms(x, g):
    ms = jnp.mean(x * x, axis=-1, keepdims=True)
    return x * lax.rsqrt(ms + EPS) * g


def _gelu(x):
    c = math.sqrt(2.0 / math.pi)
    t = jnp.tanh(x * (c + (c * 0.044715) * (x * x)))
    hx = 0.5 * x
    return hx + hx * t


def _sigmoid(x):
    return 0.5 * jnp.tanh(0.5 * x) + 0.5


def _dot(a, b):
    return jnp.dot(a, b, preferred_element_type=F32)


def _dot_nt(a, b):
    return lax.dot_general(a, b, (((1,), (1,)), ((), ())), preferred_element_type=F32)


def _layer_spec(stacked, layer):
    tail = (0,) * (stacked.ndim - 1)
    return pl.BlockSpec((None,) + stacked.shape[1:], lambda *_: (layer,) + tail,
                        pipeline_mode=pl.Buffered(1))


def _whole_spec(a):
    zeros = (0,) * a.ndim
    return pl.BlockSpec(a.shape, lambda *_: zeros, pipeline_mode=pl.Buffered(1))


def _cast_plan(stacked_f32, layer, n_steps, step_of=lambda i: i):
    in_specs, out_specs, out_shapes = [], [], []
    for a in stacked_f32:
        r, c = a.shape[1:]
        rb = r // n_steps
        assert rb * n_steps == r and rb % BF16_SUBLANES == 0
        in_specs.append(pl.BlockSpec((None, rb, c), lambda *g: (layer, step_of(*g), 0)))
        out_specs.append(pl.BlockSpec((rb, c), lambda *g: (step_of(*g), 0)))
        out_shapes.append(jax.ShapeDtypeStruct((r, c), BF16))
    return in_specs, out_specs, out_shapes


def _cast_rows(srcs, dsts):
    for src, dst in zip(srcs, dsts):
        dst[...] = src[...].astype(BF16)


def _mixer_kernel(blocks_per_seq, n_cast, *refs):
    (sinks_ref, x_ref, g_ref, w_in_ref, w_sp_ref, b_sp_ref, g_sgu_ref, w_pool_ref, pool_scale_ref,
     w_branch_ref, w_out_ref) = refs[:N_MIXER_IN]
    cast_src = refs[N_MIXER_IN:N_MIXER_IN + n_cast]
    o_ref = refs[N_MIXER_IN + n_cast]
    cast_dst = refs[N_MIXER_IN + n_cast + 1:N_MIXER_IN + 2 * n_cast + 1]
    kbuf, vbuf, cbuf, ya_ref, yb_ref, yc_ref = refs[N_MIXER_IN + 2 * n_cast + 1:]
    tb = x_ref.shape[0]
    n_qblk = tb // ATT_BLOCK
    step = pl.program_id(0)
    first = (step % blocks_per_seq) == 0
    seq_pos0 = (step % blocks_per_seq) * tb
    half = tb // 2
    blocks_per_gate = n_qblk // (D_MODEL // GATE_COLS)

    @pl.when(first)
    def _():
        kbuf[0:ATT_BLOCK, :] = jnp.zeros((ATT_BLOCK, A_KV_DIM), F32)
        vbuf[0:ATT_BLOCK, :] = jnp.zeros((ATT_BLOCK, A_KV_DIM), F32)
        cbuf[0:POOL_HALO, :] = jnp.zeros((POOL_HALO, POOL_DIM), F32)

    hb_halves = [_rms(x_ref[r * half:(r + 1) * half, :], g_ref[0:1, :]).astype(BF16) for r in range(2)]
    qkv = jnp.concatenate([_dot(hb_r, w_in_ref[:, OFF_Q:OFF_SU]) for hb_r in hb_halves], axis=0)
    hb = jnp.concatenate(hb_halves, axis=0)

    q = qkv[:, 0:A_Q_DIM] * (LOG2_E / math.sqrt(HEAD_DIM))
    k_new = qkv[:, A_Q_DIM:A_Q_DIM + A_KV_DIM]
    v_new = qkv[:, A_Q_DIM + A_KV_DIM:A_Q_DIM + 2 * A_KV_DIM]
    kbuf[ATT_BLOCK:ATT_BLOCK + tb, :] = k_new
    vbuf[ATT_BLOCK:ATT_BLOCK + tb, :] = v_new
    k_all = kbuf[...]
    v_all = vbuf[...]
    k_sw = pltpu.roll(k_all, HEAD_DIM, 1)
    lo_kv = lax.broadcasted_iota(jnp.int32, k_all.shape, 1) < HEAD_DIM
    k_dup = (jnp.where(lo_kv, k_all, k_sw).astype(BF16), jnp.where(lo_kv, k_sw, k_all).astype(BF16))
    v_t = v_all.T.astype(BF16)
    ones_rows = jnp.ones((BF16_SUBLANES, 2 * ATT_BLOCK), BF16)
    kbuf[0:ATT_BLOCK, :] = k_new[tb - ATT_BLOCK:tb, :]
    vbuf[0:ATT_BLOCK, :] = v_new[tb - ATT_BLOCK:tb, :]

    lo_q = lax.broadcasted_iota(jnp.int32, (ATT_BLOCK, LANES), 1) < HEAD_DIM
    k_row = lax.broadcasted_iota(jnp.int32, (2 * ATT_BLOCK, ATT_BLOCK), 0)
    k_chunk = k_row // CHUNK
    q_chunk = lax.broadcasted_iota(jnp.int32, (2 * ATT_BLOCK, ATT_BLOCK), 1) // CHUNK
    band = (k_chunk >= q_chunk) & (k_chunk <= q_chunk + 2)
    k_min = jnp.where(first, ATT_BLOCK, 0)

    def scores(j):
        rows = slice(j * ATT_BLOCK, (j + 1) * ATT_BLOCK)
        keys = slice(j * ATT_BLOCK, (j + 2) * ATT_BLOCK)
        out = []
        for h in range(A_KV_HEADS):
            q_stack = []
            for p in (2 * h, 2 * h + 1):
                qp = q[rows, p * LANES:(p + 1) * LANES]
                q_stack.append(jnp.where(lo_q, qp, 0.0).astype(BF16))
                q_stack.append(jnp.where(lo_q, 0.0, qp).astype(BF16))
            out.append(_dot_nt(k_dup[h][keys], jnp.concatenate(q_stack, axis=0)))
        return out

    def attend(j, s_heads):
        rows = slice(j * ATT_BLOCK, (j + 1) * ATT_BLOCK)
        keys = slice(j * ATT_BLOCK, (j + 2) * ATT_BLOCK)
        valid = (band & (k_row >= k_min)) if j == 0 else band
        for h in range(A_KV_HEADS):
            probs, sink_w = [], []
            for a in range(A_GROUP):
                sink = sinks_ref[h * A_GROUP + a] * LOG2_E
                s = jnp.where(valid, s_heads[h][:, a * ATT_BLOCK:(a + 1) * ATT_BLOCK], NEG_INF)
                m = jnp.maximum(jnp.max(s, axis=0, keepdims=True), sink)
                probs.append(jnp.exp2(s - m).astype(BF16))
                sink_w.append(jnp.exp2(sink - m))
            v_aug = jnp.concatenate([v_t[h * HEAD_DIM:(h + 1) * HEAD_DIM, keys], ones_rows], axis=0)
            o_aug = _dot(v_aug, jnp.concatenate(probs, axis=1))
            denom = o_aug[HEAD_DIM:HEAD_DIM + 1, :] + jnp.concatenate(sink_w, axis=1)
            o_t = o_aug[0:HEAD_DIM, :] * (1.0 / denom)
            for pi in range(2):
                pair_t = jnp.concatenate(
                    [o_t[:, (2 * pi + i) * ATT_BLOCK:(2 * pi + i + 1) * ATT_BLOCK] for i in range(2)], axis=0)
                p = 2 * h + pi
                ya_ref[rows, p * LANES:(p + 1) * LANES] = pair_t.T.astype(BF16)

    def gate0_chunk(c):
        c0 = OFF_GATE + c * GATE_COLS
        return _dot(hb, w_in_ref[:, c0:c0 + GATE_COLS])

    s_cur = scores(0)
    gate0 = [gate0_chunk(0)]
    for j in range(n_qblk):
        s_next = scores(j + 1) if j + 1 < n_qblk else None
        if (j + 1) % blocks_per_gate == 0 and (j + 1) // blocks_per_gate < D_MODEL // GATE_COLS:
            gate0.append(gate0_chunk((j + 1) // blocks_per_gate))
        attend(j, s_cur)
        s_cur = s_next
    gates = [_sigmoid(jnp.concatenate(gate0, axis=1))]

    _cast_rows(cast_src, cast_dst)

    uv = _dot(hb, w_in_ref[:, OFF_SU:OFF_PC])
    pc = _dot(hb, w_in_ref[:, OFF_PC:OFF_GATE])
    gates.append(_sigmoid(_dot(hb, w_in_ref[:, OFF_GATE + D_MODEL:OFF_GATE + 2 * D_MODEL])))
    u = _gelu(uv[:, 0:SGU_DIM])
    vn = _rms(_gelu(uv[:, SGU_DIM:2 * SGU_DIM]), g_sgu_ref[...]).astype(BF16)
    sp_row = lax.broadcasted_iota(jnp.int32, (SGU_CHUNK, SGU_CHUNK), 0) // CHUNK
    sp_col = lax.broadcasted_iota(jnp.int32, (SGU_CHUNK, SGU_CHUNK), 1) // CHUNK
    w_sp = [jnp.where(sp_col <= sp_row, w_sp_ref[g], 0.0).astype(BF16) for g in range(SGU_GROUPS)]
    for c in range(tb // SGU_CHUNK):
        rows = slice(c * SGU_CHUNK, (c + 1) * SGU_CHUNK)
        for g in range(SGU_GROUPS):
            cols = slice(g * SGU_GROUP_DIM, (g + 1) * SGU_GROUP_DIM)
            sp = _dot(w_sp[g], vn[rows, cols]) + b_sp_ref[:, cols]
            yb_ref[rows, cols] = (u[rows, cols] * sp).astype(BF16)

    gates.append(_sigmoid(_dot(hb, w_in_ref[:, OFF_GATE + 2 * D_MODEL:OFF_GATE + 3 * D_MODEL])))
    cbuf[POOL_HALO:POOL_HALO + tb, :] = pc
    t_pos = seq_pos0 + lax.broadcasted_iota(jnp.int32, (tb, POOL_GROUP_DIM), 0)
    for g, w in enumerate(POOL_WINDOWS):
        cols = slice(g * POOL_GROUP_DIM, (g + 1) * POOL_GROUP_DIM)
        acc = cbuf[:, cols]
        lag = 1
        while lag < w:
            acc = acc + pltpu.roll(acc, lag, 0)
            lag *= 2
        cnt = jnp.minimum(t_pos + 1, w).astype(F32)
        pooled = (acc[POOL_HALO:, :] / cnt - pc[:, cols]).astype(BF16)
        mixed = _dot(pooled, w_pool_ref[cols, :]) * pool_scale_ref[:, cols]
        yc_ref[:, cols] = mixed.astype(BF16)
    cbuf[0:POOL_HALO, :] = pc[tb - POOL_HALO:tb, :]

    merged = []
    for r in range(2):
        rows = slice(r * half, (r + 1) * half)
        acc = None
        for n, y_ref in enumerate((ya_ref, yb_ref, yc_ref)):
            proj = _dot(y_ref[rows, :], w_branch_ref[n * BRANCH_DIM:(n + 1) * BRANCH_DIM, :])
            term = gates[n][rows] * proj
            acc = term if acc is None else acc + term
        merged.append(acc.astype(BF16))
    merged = jnp.concatenate(merged, axis=0)
    start = 0
    for size in OUT_PARTS:
        rows = slice(start, start + size)
        out = _dot(merged[rows], w_out_ref[...])
        o_ref[rows, :] = x_ref[rows, :] + _rms(out, g_ref[1:2, :])
        start += size


def _mixer(layer, x, sinks, g_norm, w_in, w_sp, b_sp_full, g_sgu, w_pool, pool_scale, w_branch, w_out,
           to_cast, seq_len):
    t, d = x.shape
    tb = TOKEN_BLOCK
    assert (tb // ATT_BLOCK) % (D_MODEL // GATE_COLS) == 0 and tb % 2 == 0 and sum(OUT_PARTS) == tb
    n_steps = t // tb
    kern = functools.partial(_mixer_kernel, seq_len // tb, len(to_cast))
    cast_in, cast_out, cast_shapes = _cast_plan(to_cast, layer, n_steps)
    x_spec = pl.BlockSpec((tb, d), lambda i: (i, 0))
    outs = pl.pallas_call(
        kern,
        out_shape=[jax.ShapeDtypeStruct((t, d), F32)] + cast_shapes,
        grid=(n_steps,),
        in_specs=[pl.BlockSpec(memory_space=pltpu.SMEM), x_spec, _layer_spec(g_norm, layer),
                  _whole_spec(w_in), _layer_spec(w_sp, layer), _layer_spec(b_sp_full, layer),
                  _layer_spec(g_sgu, layer), _whole_spec(w_pool), _layer_spec(pool_scale, layer),
                  _whole_spec(w_branch), _whole_spec(w_out)] + cast_in,
        out_specs=[x_spec] + cast_out,
        scratch_shapes=[
            pltpu.VMEM((ATT_BLOCK + tb, A_KV_DIM), F32),
            pltpu.VMEM((ATT_BLOCK + tb, A_KV_DIM), F32),
            pltpu.VMEM((POOL_HALO + tb, POOL_DIM), F32),
            pltpu.VMEM((tb, BRANCH_DIM), BF16),
            pltpu.VMEM((tb, BRANCH_DIM), BF16),
            pltpu.VMEM((tb, BRANCH_DIM), BF16),
        ],
        compiler_params=pltpu.CompilerParams(
            dimension_semantics=("arbitrary",), vmem_limit_bytes=VMEM_LIMIT_BYTES),
        name="mixer",
    )(sinks, x, g_norm, w_in, w_sp, b_sp_full, g_sgu, w_pool, pool_scale, w_branch, w_out, *to_cast)
    return outs[0], outs[1:]


def _mem_kv_kernel(n_cast, mem_ref, g_ref, w_ref, *refs):
    cast_src = refs[:n_cast]
    k_ref, v_ref = refs[n_cast:n_cast + 2]
    cast_dst = refs[n_cast + 2:2 * n_cast + 2]
    w_bf = refs[2 * n_cast + 2]
    _cast_rows(cast_src, cast_dst)

    @pl.when(pl.program_id(1) == 0)
    def _():
        w_bf[...] = w_ref[...].astype(BF16)

    mem_n = _rms(mem_ref[...], g_ref[...]).astype(BF16)
    kv = _dot(mem_n, w_bf[...])
    k_ref[...] = kv[:, 0:MEM_DIM].astype(BF16)
    v_ref[...] = kv[:, MEM_DIM:2 * MEM_DIM].astype(BF16)


def _mem_kv(mem, g_mem, w_kv, to_cast):
    b = mem.shape[0]
    out = jax.ShapeDtypeStruct((DEPTH, b, N_MEM, MEM_DIM), BF16)
    kv_spec = pl.BlockSpec((None, None, N_MEM, MEM_DIM), lambda l, i: (l, i, 0, 0))
    cast_in, cast_out, cast_shapes = _cast_plan(to_cast, 0, DEPTH * b, lambda l, i: l * b + i)
    outs = pl.pallas_call(
        functools.partial(_mem_kv_kernel, len(to_cast)),
        out_shape=[out, out] + cast_shapes,
        grid=(DEPTH, b),
        in_specs=[
            pl.BlockSpec((None, N_MEM, D_MODEL), lambda l, i: (i, 0, 0)),
            pl.BlockSpec((None, 1, D_MODEL), lambda l, i: (l, 0, 0)),
            pl.BlockSpec((None, D_MODEL, 2 * MEM_DIM), lambda l, i: (l, 0, 0)),
        ] + cast_in,
        out_specs=[kv_spec, kv_spec] + cast_out,
        scratch_shapes=[pltpu.VMEM((D_MODEL, 2 * MEM_DIM), BF16)],
        compiler_params=pltpu.CompilerParams(
            dimension_semantics=("arbitrary", "arbitrary"), vmem_limit_bytes=VMEM_LIMIT_BYTES),
        name="mem_kv",
    )(mem, g_mem, w_kv, *to_cast)
    return outs[0], outs[1], outs[2:]


def _mem_attn_kernel(x_ref, g_ref, w_q_ref, k_ref, v_ref, w_o_ref, o_ref, om_ref):
    rows_per_part = x_ref.shape[0] // MEM_PARTS
    parts = [slice(p * rows_per_part, (p + 1) * rows_per_part) for p in range(MEM_PARTS)]

    def query(rows):
        hb = _rms(x_ref[rows, :], g_ref[2:3, :]).astype(BF16)
        return (_dot(hb, w_q_ref[...]) * (LOG2_E / math.sqrt(MEM_HEAD_DIM))).astype(BF16)

    def scores(q):
        return [_dot_nt(q[:, h * MEM_HEAD_DIM:(h + 1) * MEM_HEAD_DIM],
                        k_ref[:, h * MEM_HEAD_DIM:(h + 1) * MEM_HEAD_DIM]) for h in range(MEM_HEADS)]

    ones = jnp.ones((N_MEM, MEM_HEAD_DIM), BF16)
    v_aug = [jnp.concatenate([v_ref[:, h * MEM_HEAD_DIM:(h + 1) * MEM_HEAD_DIM], ones], axis=1)
             for h in range(MEM_HEADS)]

    def attend(rows, s_heads):
        for h, s in enumerate(s_heads):
            cols = slice(h * MEM_HEAD_DIM, (h + 1) * MEM_HEAD_DIM)
            m = jnp.max(s, axis=-1, keepdims=True)
            o_aug = _dot(jnp.exp2(s - m).astype(BF16), v_aug[h])
            om_ref[rows, cols] = (o_aug[:, :MEM_HEAD_DIM] / o_aug[:, MEM_HEAD_DIM:]).astype(BF16)
        ym = _dot(om_ref[rows, :], w_o_ref[...])
        o_ref[rows, :] = x_ref[rows, :] + _rms(ym, g_ref[3:4, :])

    pending = []
    for t in range(MEM_PARTS + MEM_LAG):
        if t < MEM_PARTS:
            pending.append(scores(query(parts[t])))
        if t >= MEM_LAG:
            attend(parts[t - MEM_LAG], pending[t - MEM_LAG])


def _mem_attn(layer, x, g_norm, w_q, k_mem, v_mem, w_o, seq_len):
    t, d = x.shape
    tb = WIDE_BLOCK
    bps = seq_len // tb
    kv_spec = pl.BlockSpec((None, None, N_MEM, MEM_DIM), lambda i: (layer, i // bps, 0, 0))
    return pl.pallas_call(
        _mem_attn_kernel,
        out_shape=jax.ShapeDtypeStruct((t, d), F32),
        grid=(t // tb,),
        in_specs=[
            pl.BlockSpec((tb, d), lambda i: (i, 0)),
            _layer_spec(g_norm, layer),
            _whole_spec(w_q),
            kv_spec,
            kv_spec,
            _whole_spec(w_o),
        ],
        out_specs=pl.BlockSpec((tb, d), lambda i: (i, 0)),
        scratch_shapes=[pltpu.VMEM((tb, MEM_DIM), BF16)],
        compiler_params=pltpu.CompilerParams(
            dimension_semantics=("arbitrary",), vmem_limit_bytes=VMEM_LIMIT_BYTES),
        name="mem_attn",
    )(x, g_norm, w_q, k_mem, v_mem, w_o)


def _ffn_kernel(n_cast, x_ref, g_ref, w_up_ref, w_down_ref, *refs):
    cast_src = refs[:n_cast]
    o_ref = refs[n_cast]
    cast_dst = refs[n_cast + 1:]
    rows_per_part = x_ref.shape[0] // FF_PARTS
    parts = [slice(p * rows_per_part, (p + 1) * rows_per_part) for p in range(FF_PARTS)]
    hbs, yfs = [], [None] * FF_PARTS
    for c in range(D_FF // FF_CHUNK):
        cols = slice(c * FF_CHUNK, (c + 1) * FF_CHUNK)
        if c == 1:
            _cast_rows(cast_src, cast_dst)
        for p, rows in enumerate(parts):
            if c == 0:
                hbs.append(_rms(x_ref[rows, :], g_ref[4:5, :]).astype(BF16))
            up = jnp.maximum(_dot(hbs[p], w_up_ref[:, cols]), 0.0)
            part = _dot((up * up).astype(BF16), w_down_ref[cols, :])
            yfs[p] = part if yfs[p] is None else yfs[p] + part
    for p, rows in enumerate(parts):
        o_ref[rows, :] = x_ref[rows, :] + _rms(yfs[p], g_ref[5:6, :])


def _ffn(layer, x, g_norm, w_up, w_down, to_cast, cast_layer):
    t, d = x.shape
    tb = WIDE_BLOCK
    n_steps = t // tb
    cast_in, cast_out, cast_shapes = _cast_plan(to_cast, cast_layer, n_steps)
    x_spec = pl.BlockSpec((tb, d), lambda i: (i, 0))
    outs = pl.pallas_call(
        functools.partial(_ffn_kernel, len(to_cast)),
        out_shape=[jax.ShapeDtypeStruct((t, d), F32)] + cast_shapes,
        grid=(n_steps,),
        in_specs=[x_spec, _layer_spec(g_norm, layer), _whole_spec(w_up), _whole_spec(w_down)] + cast_in,
        out_specs=[x_spec] + cast_out,
        compiler_params=pltpu.CompilerParams(
            dimension_semantics=("arbitrary",), vmem_limit_bytes=VMEM_LIMIT_BYTES),
        name="ffn",
    )(x, g_norm, w_up, w_down, *to_cast)
    return outs[0], outs[1:]


def kernel(x, mem, g_norm, g_mem, w_in, attn_sinks, w_spatial, b_spatial, g_sgu, w_pool, pool_scale,
           w_branch, w_out, w_q_mem, w_kv_mem, w_o_mem, w_up, w_down):
    b, s, d = x.shape
    assert d == D_MODEL and s % TOKEN_BLOCK == 0 and TOKEN_BLOCK % ATT_BLOCK == 0 and s % WIDE_BLOCK == 0
    xt = x.reshape(b * s, d)

    b_sp_full = jnp.repeat(jnp.swapaxes(b_spatial, 1, 2), SGU_GROUP_DIM, axis=2)
    g_sgu3 = g_sgu.reshape(DEPTH, 1, SGU_DIM)
    pool_scale3 = pool_scale.reshape(DEPTH, 1, POOL_DIM)
    mixer_w = (w_in, w_pool.reshape(DEPTH, POOL_DIM, POOL_GROUP_DIM),
               w_branch.reshape(DEPTH, N_BRANCH * BRANCH_DIM, D_MODEL), w_out)
    later_w = (w_q_mem, w_o_mem, w_up, w_down)

    k_mem, v_mem, mixer_b = _mem_kv(mem, g_mem.reshape(DEPTH, 1, D_MODEL), w_kv_mem, mixer_w)
    for l in range(DEPTH):
        w_in_b, w_pool_b, w_branch_b, w_out_b = mixer_b
        xt, (w_q_b, w_o_b, w_up_b, w_down_b) = _mixer(
            l, xt, attn_sinks[l], g_norm, w_in_b, w_spatial, b_sp_full, g_sgu3, w_pool_b, pool_scale3,
            w_branch_b, w_out_b, later_w, s)
        xt = _mem_attn(l, xt, g_norm, w_q_b, k_mem, v_mem, w_o_b, s)
        xt, mixer_b = _ffn(l, xt, g_norm, w_up_b, w_down_b, mixer_w if l + 1 < DEPTH else (), l + 1)
    return xt.reshape(b, s, d)
```

```python
import functools
import math

import jax
import jax.numpy as jnp
from jax import lax
from jax.experimental import pallas as pl
from jax.experimental.pallas import tpu as pltpu

D_MODEL = 1024
DEPTH = 4
CHUNK = 64
N_MEM = 256
EPS = 1e-6
NEG_INF = -1e30
LOG2_E = math.log2(math.e)
N_NORMS = 6

BRANCH_DIM = D_MODEL // 2
N_BRANCH = 3
HEAD_DIM = 64
A_Q_HEADS = BRANCH_DIM // HEAD_DIM
A_KV_HEADS = 2
A_GROUP = A_Q_HEADS // A_KV_HEADS
A_Q_DIM = A_Q_HEADS * HEAD_DIM
A_KV_DIM = A_KV_HEADS * HEAD_DIM
ATT_BLOCK = 128
SGU_CHUNK = 128
SGU_GROUPS = 4
SGU_DIM = BRANCH_DIM
SGU_GROUP_DIM = SGU_DIM // SGU_GROUPS
POOL_WINDOWS = (2, 4, 8, 16)
POOL_GROUPS = 4
POOL_DIM = BRANCH_DIM
POOL_GROUP_DIM = POOL_DIM // POOL_GROUPS
POOL_HALO = 16
IN_DIM = A_Q_DIM + 2 * A_KV_DIM + 2 * SGU_DIM + POOL_DIM + N_BRANCH * D_MODEL
MEM_HEADS = 4
MEM_HEAD_DIM = 128
MEM_DIM = MEM_HEADS * MEM_HEAD_DIM
D_FF = 4 * D_MODEL

OFF_Q = 0
OFF_K = OFF_Q + A_Q_DIM
OFF_V = OFF_K + A_KV_DIM
OFF_SU = OFF_V + A_KV_DIM
OFF_SV = OFF_SU + SGU_DIM
OFF_PC = OFF_SV + SGU_DIM
OFF_GATE = OFF_PC + POOL_DIM

LANES = 128
BF16_SUBLANES = 16
TOKEN_BLOCK = 512
OUT_PARTS = (256, 256)
GATE_COLS = 256
WIDE_BLOCK = 1024
MEM_BLOCK = 2048
FF_CHUNK = 1024
FF_PARTS = 4
MEM_PARTS = 2
MEM_LAG = 2
VMEM_LIMIT_BYTES = 56 * 1024 * 1024
N_MIXER_IN = 11

BF16 = jnp.bfloat16
F32 = jnp.float32


def _rms(x, g):
    ms = jnp.mean(x * x, axis=-1, keepdims=True)
    return x * lax.rsqrt(ms + EPS) * g


def _gelu(x):
    c = math.sqrt(2.0 / math.pi)
    t = jnp.tanh(x * (c + (c * 0.044715) * (x * x)))
    hx = 0.5 * x
    return hx + hx * t


def _sigmoid(x):
    return 0.5 * jnp.tanh(0.5 * x) + 0.5


def _dot(a, b):
    return jnp.dot(a, b, preferred_element_type=F32)


def _dot_nt(a, b):
    return lax.dot_general(a, b, (((1,), (1,)), ((), ())), preferred_element_type=F32)


def _layer_spec(stacked, layer):
    tail = (0,) * (stacked.ndim - 1)
    return pl.BlockSpec((None,) + stacked.shape[1:], lambda *_: (layer,) + tail,
                        pipeline_mode=pl.Buffered(1))


def _whole_spec(a):
    zeros = (0,) * a.ndim
    return pl.BlockSpec(a.shape, lambda *_: zeros, pipeline_mode=pl.Buffered(1))


def _cast_plan(stacked_f32, layer, n_steps, step_of=lambda i: i):
    in_specs, out_specs, out_shapes = [], [], []
    for a in stacked_f32:
        r, c = a.shape[1:]
        rb = r // n_steps
        assert rb * n_steps == r and rb % BF16_SUBLANES == 0
        in_specs.append(pl.BlockSpec((None, rb, c), lambda *g: (layer, step_of(*g), 0)))
        out_specs.append(pl.BlockSpec((rb, c), lambda *g: (step_of(*g), 0)))
        out_shapes.append(jax.ShapeDtypeStruct((r, c), BF16))
    return in_specs, out_specs, out_shapes


def _cast_rows(srcs, dsts):
    for src, dst in zip(srcs, dsts):
        dst[...] = src[...].astype(BF16)


def _mixer_kernel(blocks_per_seq, n_cast, *refs):
    (sinks_ref, x_ref, g_ref, w_in_ref, w_sp_ref, b_sp_ref, g_sgu_ref, w_pool_ref, pool_scale_ref,
     w_branch_ref, w_out_ref) = refs[:N_MIXER_IN]
    cast_src = refs[N_MIXER_IN:N_MIXER_IN + n_cast]
    o_ref = refs[N_MIXER_IN + n_cast]
    cast_dst = refs[N_MIXER_IN + n_cast + 1:N_MIXER_IN + 2 * n_cast + 1]
    kbuf, vbuf, cbuf, ya_ref, yb_ref, yc_ref = refs[N_MIXER_IN + 2 * n_cast + 1:]
    tb = x_ref.shape[0]
    n_qblk = tb // ATT_BLOCK
    step = pl.program_id(0)
    first = (step % blocks_per_seq) == 0
    seq_pos0 = (step % blocks_per_seq) * tb
    half = tb // 2
    blocks_per_gate = n_qblk // (D_MODEL // GATE_COLS)

    @pl.when(first)
    def _():
        kbuf[0:ATT_BLOCK, :] = jnp.zeros((ATT_BLOCK, A_KV_DIM), F32)
        vbuf[0:ATT_BLOCK, :] = jnp.zeros((ATT_BLOCK, A_KV_DIM), F32)
        cbuf[0:POOL_HALO, :] = jnp.zeros((POOL_HALO, POOL_DIM), F32)

    hb_halves = [_rms(x_ref[r * half:(r + 1) * half, :], g_ref[0:1, :]).astype(BF16) for r in range(2)]
    qkv = jnp.concatenate([_dot(hb_r, w_in_ref[:, OFF_Q:OFF_SU]) for hb_r in hb_halves], axis=0)
    hb = jnp.concatenate(hb_halves, axis=0)

    q = qkv[:, 0:A_Q_DIM] * (LOG2_E / math.sqrt(HEAD_DIM))
    k_new = qkv[:, A_Q_DIM:A_Q_DIM + A_KV_DIM]
    v_new = qkv[:, A_Q_DIM + A_KV_DIM:A_Q_DIM + 2 * A_KV_DIM]
    kbuf[ATT_BLOCK:ATT_BLOCK + tb, :] = k_new
    vbuf[ATT_BLOCK:ATT_BLOCK + tb, :] = v_new
    k_all = kbuf[...]
    v_all = vbuf[...]
    k_sw = pltpu.roll(k_all, HEAD_DIM, 1)
    lo_kv = lax.broadcasted_iota(jnp.int32, k_all.shape, 1) < HEAD_DIM
    k_dup = (jnp.where(lo_kv, k_all, k_sw).astype(BF16), jnp.where(lo_kv, k_sw, k_all).astype(BF16))
    v_t = v_all.T.astype(BF16)
    ones_rows = jnp.ones((BF16_SUBLANES, 2 * ATT_BLOCK), BF16)
    kbuf[0:ATT_BLOCK, :] = k_new[tb - ATT_BLOCK:tb, :]
    vbuf[0:ATT_BLOCK, :] = v_new[tb - ATT_BLOCK:tb, :]

    lo_q = lax.broadcasted_iota(jnp.int32, (ATT_BLOCK, LANES), 1) < HEAD_DIM
    k_row = lax.broadcasted_iota(jnp.int32, (2 * ATT_BLOCK, ATT_BLOCK), 0)
    k_chunk = k_row // CHUNK
    q_chunk = lax.broadcasted_iota(jnp.int32, (2 * ATT_BLOCK, ATT_BLOCK), 1) // CHUNK
    band = (k_chunk >= q_chunk) & (k_chunk <= q_chunk + 2)
    k_min = jnp.where(first, ATT_BLOCK, 0)

    def scores(j):
        rows = slice(j * ATT_BLOCK, (j + 1) * ATT_BLOCK)
        keys = slice(j * ATT_BLOCK, (j + 2) * ATT_BLOCK)
        out = []
        for h in range(A_KV_HEADS):
            q_stack = []
            for p in (2 * h, 2 * h + 1):
                qp = q[rows, p * LANES:(p + 1) * LANES]
                q_stack.append(jnp.where(lo_q, qp, 0.0).astype(BF16))
                q_stack.append(jnp.where(lo_q, 0.0, qp).astype(BF16))
            out.append(_dot_nt(k_dup[h][keys], jnp.concatenate(q_stack, axis=0)))
        return out

    def attend(j, s_heads):
        rows = slice(j * ATT_BLOCK, (j + 1) * ATT_BLOCK)
        keys = slice(j * ATT_BLOCK, (j + 2) * ATT_BLOCK)
        valid = (band & (k_row >= k_min)) if j == 0 else band
        for h in range(A_KV_HEADS):
            probs, sink_w = [], []
            for a in range(A_GROUP):
                sink = sinks_ref[h * A_GROUP + a] * LOG2_E
                s = jnp.where(valid, s_heads[h][:, a * ATT_BLOCK:(a + 1) * ATT_BLOCK], NEG_INF)
                m = jnp.maximum(jnp.max(s, axis=0, keepdims=True), sink)
                probs.append(jnp.exp2(s - m).astype(BF16))
                sink_w.append(jnp.exp2(sink - m))
            v_aug = jnp.concatenate([v_t[h * HEAD_DIM:(h + 1) * HEAD_DIM, keys], ones_rows], axis=0)
            o_aug = _dot(v_aug, jnp.concatenate(probs, axis=1))
            denom = o_aug[HEAD_DIM:HEAD_DIM + 1, :] + jnp.concatenate(sink_w, axis=1)
            o_t = o_aug[0:HEAD_DIM, :] * (1.0 / denom)
            for pi in range(2):
                pair_t = jnp.concatenate(
                    [o_t[:, (2 * pi + i) * ATT_BLOCK:(2 * pi + i + 1) * ATT_BLOCK] for i in range(2)], axis=0)
                p = 2 * h + pi
                ya_ref[rows, p * LANES:(p + 1) * LANES] = pair_t.T.astype(BF16)

    def gate0_chunk(c):
        c0 = OFF_GATE + c * GATE_COLS
        return _dot(hb, w_in_ref[:, c0:c0 + GATE_COLS])

    s_cur = scores(0)
    gate0 = [gate0_chunk(0)]
    for j in range(n_qblk):
        s_next = scores(j + 1) if j + 1 < n_qblk else None
        if (j + 1) % blocks_per_gate == 0 and (j + 1) // blocks_per_gate < D_MODEL // GATE_COLS:
            gate0.append(gate0_chunk((j + 1) // blocks_per_gate))
        attend(j, s_cur)
        s_cur = s_next
    gates = [_sigmoid(jnp.concatenate(gate0, axis=1))]

    _cast_rows(cast_src, cast_dst)

    uv = _dot(hb, w_in_ref[:, OFF_SU:OFF_PC])
    pc = _dot(hb, w_in_ref[:, OFF_PC:OFF_GATE])
    gates.append(_sigmoid(_dot(hb, w_in_ref[:, OFF_GATE + D_MODEL:OFF_GATE + 2 * D_MODEL])))
    u = _gelu(uv[:, 0:SGU_DIM])
    vn = _rms(_gelu(uv[:, SGU_DIM:2 * SGU_DIM]), g_sgu_ref[...]).astype(BF16)
    sp_row = lax.broadcasted_iota(jnp.int32, (SGU_CHUNK, SGU_CHUNK), 0) // CHUNK
    sp_col = lax.broadcasted_iota(jnp.int32, (SGU_CHUNK, SGU_CHUNK), 1) // CHUNK
    w_sp = [jnp.where(sp_col <= sp_row, w_sp_ref[g], 0.0).astype(BF16) for g in range(SGU_GROUPS)]
    for c in range(tb // SGU_CHUNK):
        rows = slice(c * SGU_CHUNK, (c + 1) * SGU_CHUNK)
        for g in range(SGU_GROUPS):
            cols = slice(g * SGU_GROUP_DIM, (g + 1) * SGU_GROUP_DIM)
            sp = _dot(w_sp[g], vn[rows, cols]) + b_sp_ref[:, cols]
            yb_ref[rows, cols] = (u[rows, cols] * sp).astype(BF16)

    gates.append(_sigmoid(_dot(hb, w_in_ref[:, OFF_GATE + 2 * D_MODEL:OFF_GATE + 3 * D_MODEL])))
    cbuf[POOL_HALO:POOL_HALO + tb, :] = pc
    t_pos = seq_pos0 + lax.broadcasted_iota(jnp.int32, (tb, POOL_GROUP_DIM), 0)
    for g, w in enumerate(POOL_WINDOWS):
        cols = slice(g * POOL_GROUP_DIM, (g + 1) * POOL_GROUP_DIM)
        acc = cbuf[:, cols]
        lag = 1
        while lag < w:
            acc = acc + pltpu.roll(acc, lag, 0)
            lag *= 2
        cnt = jnp.minimum(t_pos + 1, w).astype(F32)
        pooled = (acc[POOL_HALO:, :] / cnt - pc[:, cols]).astype(BF16)
        mixed = _dot(pooled, w_pool_ref[cols, :]) * pool_scale_ref[:, cols]
        yc_ref[:, cols] = mixed.astype(BF16)
    cbuf[0:POOL_HALO, :] = pc[tb - POOL_HALO:tb, :]

    merged = []
    for r in range(2):
        rows = slice(r * half, (r + 1) * half)
        acc = None
        for n, y_ref in enumerate((ya_ref, yb_ref, yc_ref)):
            proj = _dot(y_ref[rows, :], w_branch_ref[n * BRANCH_DIM:(n + 1) * BRANCH_DIM, :])
            term = gates[n][rows] * proj
            acc = term if acc is None else acc + term
        merged.append(acc.astype(BF16))
    merged = jnp.concatenate(merged, axis=0)
    start = 0
    for size in OUT_PARTS:
        rows = slice(start, start + size)
        out = _dot(merged[rows], w_out_ref[...])
        o_ref[rows, :] = x_ref[rows, :] + _rms(out, g_ref[1:2, :])
        start += size


def _mixer(layer, x, sinks, g_norm, w_in, w_sp, b_sp_full, g_sgu, w_pool, pool_scale, w_branch, w_out,
           to_cast, seq_len):
    t, d = x.shape
    tb = TOKEN_BLOCK
    assert (tb // ATT_BLOCK) % (D_MODEL // GATE_COLS) == 0 and tb % 2 == 0 and sum(OUT_PARTS) == tb
    n_steps = t // tb
    kern = functools.partial(_mixer_kernel, seq_len // tb, len(to_cast))
    cast_in, cast_out, cast_shapes = _cast_plan(to_cast, layer, n_steps)
    x_spec = pl.BlockSpec((tb, d), lambda i: (i, 0))
    outs = pl.pallas_call(
        kern,
        out_shape=[jax.ShapeDtypeStruct((t, d), F32)] + cast_shapes,
        grid=(n_steps,),
        in_specs=[pl.BlockSpec(memory_space=pltpu.SMEM), x_spec, _layer_spec(g_norm, layer),
                  _whole_spec(w_in), _layer_spec(w_sp, layer), _layer_spec(b_sp_full, layer),
                  _layer_spec(g_sgu, layer), _whole_spec(w_pool), _layer_spec(pool_scale, layer),
                  _whole_spec(w_branch), _whole_spec(w_out)] + cast_in,
        out_specs=[x_spec] + cast_out,
        scratch_shapes=[
            pltpu.VMEM((ATT_BLOCK + tb, A_KV_DIM), F32),
            pltpu.VMEM((ATT_BLOCK + tb, A_KV_DIM), F32),
            pltpu.VMEM((POOL_HALO + tb, POOL_DIM), F32),
            pltpu.VMEM((tb, BRANCH_DIM), BF16),
            pltpu.VMEM((tb, BRANCH_DIM), BF16),
            pltpu.VMEM((tb, BRANCH_DIM), BF16),
        ],
        compiler_params=pltpu.CompilerParams(
            dimension_semantics=("arbitrary",), vmem_limit_bytes=VMEM_LIMIT_BYTES),
        name="mixer",
    )(sinks, x, g_norm, w_in, w_sp, b_sp_full, g_sgu, w_pool, pool_scale, w_branch, w_out, *to_cast)
    return outs[0], outs[1:]


def _mem_kv_kernel(n_cast, mem_ref, g_ref, w_ref, *refs):
    cast_src = refs[:n_cast]
    k_ref, v_ref = refs[n_cast:n_cast + 2]
    cast_dst = refs[n_cast + 2:2 * n_cast + 2]
    w_bf = refs[2 * n_cast + 2]
    _cast_rows(cast_src, cast_dst)

    @pl.when(pl.program_id(1) == 0)
    def _():
        w_bf[...] = w_ref[...].astype(BF16)

    mem_n = _rms(mem_ref[...], g_ref[...]).astype(BF16)
    kv = _dot(mem_n, w_bf[...])
    k_ref[...] = kv[:, 0:MEM_DIM].astype(BF16)
    v_ref[...] = kv[:, MEM_DIM:2 * MEM_DIM].astype(BF16)


def _mem_kv(mem, g_mem, w_kv, to_cast):
    b = mem.shape[0]
    out = jax.ShapeDtypeStruct((DEPTH, b, N_MEM, MEM_DIM), BF16)
    kv_spec = pl.BlockSpec((None, None, N_MEM, MEM_DIM), lambda l, i: (l, i, 0, 0))
    cast_in, cast_out, cast_shapes = _cast_plan(to_cast, 0, DEPTH * b, lambda l, i: l * b + i)
    outs = pl.pallas_call(
        functools.partial(_mem_kv_kernel, len(to_cast)),
        out_shape=[out, out] + cast_shapes,
        grid=(DEPTH, b),
        in_specs=[
            pl.BlockSpec((None, N_MEM, D_MODEL), lambda l, i: (i, 0, 0)),
            pl.BlockSpec((None, 1, D_MODEL), lambda l, i: (l, 0, 0)),
            pl.BlockSpec((None, D_MODEL, 2 * MEM_DIM), lambda l, i: (l, 0, 0)),
        ] + cast_in,
        out_specs=[kv_spec, kv_spec] + cast_out,
        scratch_shapes=[pltpu.VMEM((D_MODEL, 2 * MEM_DIM), BF16)],
        compiler_params=pltpu.CompilerParams(
            dimension_semantics=("arbitrary", "arbitrary"), vmem_limit_bytes=VMEM_LIMIT_BYTES),
        name="mem_kv",
    )(mem, g_mem, w_kv, *to_cast)
    return outs[0], outs[1], outs[2:]


def _mem_attn_kernel(x_ref, g_ref, w_q_ref, k_ref, v_ref, w_o_ref, o_ref, om_ref):
    rows_per_part = x_ref.shape[0] // MEM_PARTS
    parts = [slice(p * rows_per_part, (p + 1) * rows_per_part) for p in range(MEM_PARTS)]

    def query(rows):
        hb = _rms(x_ref[rows, :], g_ref[2:3, :]).astype(BF16)
        return (_dot(hb, w_q_ref[...]) * (LOG2_E / math.sqrt(MEM_HEAD_DIM))).astype(BF16)

    def scores(q):
        return [_dot_nt(q[:, h * MEM_HEAD_DIM:(h + 1) * MEM_HEAD_DIM],
                        k_ref[:, h * MEM_HEAD_DIM:(h + 1) * MEM_HEAD_DIM]) for h in range(MEM_HEADS)]

    ones = jnp.ones((N_MEM, MEM_HEAD_DIM), BF16)
    v_aug = [jnp.concatenate([v_ref[:, h * MEM_HEAD_DIM:(h + 1) * MEM_HEAD_DIM], ones], axis=1)
             for h in range(MEM_HEADS)]

    def attend(rows, s_heads):
        for h, s in enumerate(s_heads):
            cols = slice(h * MEM_HEAD_DIM, (h + 1) * MEM_HEAD_DIM)
            m = jnp.max(s, axis=-1, keepdims=True)
            o_aug = _dot(jnp.exp2(s - m).astype(BF16), v_aug[h])
            om_ref[rows, cols] = (o_aug[:, :MEM_HEAD_DIM] / o_aug[:, MEM_HEAD_DIM:]).astype(BF16)
        ym = _dot(om_ref[rows, :], w_o_ref[...])
        o_ref[rows, :] = x_ref[rows, :] + _rms(ym, g_ref[3:4, :])

    pending = []
    for t in range(MEM_PARTS + MEM_LAG):
        if t < MEM_PARTS:
            pending.append(scores(query(parts[t])))
        if t >= MEM_LAG:
            attend(parts[t - MEM_LAG], pending[t - MEM_LAG])


def _mem_attn(layer, x, g_norm, w_q, k_mem, v_mem, w_o, seq_len):
    t, d = x.shape
    tb = MEM_BLOCK
    bps = seq_len // tb
    kv_spec = pl.BlockSpec((None, None, N_MEM, MEM_DIM), lambda i: (layer, i // bps, 0, 0))
    return pl.pallas_call(
        _mem_attn_kernel,
        out_shape=jax.ShapeDtypeStruct((t, d), F32),
        grid=(t // tb,),
        in_specs=[
            pl.BlockSpec((tb, d), lambda i: (i, 0)),
            _layer_spec(g_norm, layer),
            _whole_spec(w_q),
            kv_spec,
            kv_spec,
            _whole_spec(w_o),
        ],
        out_specs=pl.BlockSpec((tb, d), lambda i: (i, 0)),
        scratch_shapes=[pltpu.VMEM((tb, MEM_DIM), BF16)],
        compiler_params=pltpu.CompilerParams(
            dimension_semantics=("arbitrary",), vmem_limit_bytes=VMEM_LIMIT_BYTES),
        name="mem_attn",
    )(x, g_norm, w_q, k_mem, v_mem, w_o)


def _ffn_kernel(n_cast, x_ref, g_ref, w_up_ref, w_down_ref, *refs):
    cast_src = refs[:n_cast]
    o_ref = refs[n_cast]
    cast_dst = refs[n_cast + 1:]
    rows_per_part = x_ref.shape[0] // FF_PARTS
    parts = [slice(p * rows_per_part, (p + 1) * rows_per_part) for p in range(FF_PARTS)]
    hbs, yfs = [], [None] * FF_PARTS
    for c in range(D_FF // FF_CHUNK):
        cols = slice(c * FF_CHUNK, (c + 1) * FF_CHUNK)
        if c == 1:
            _cast_rows(cast_src, cast_dst)
        for p, rows in enumerate(parts):
            if c == 0:
                hbs.append(_rms(x_ref[rows, :], g_ref[4:5, :]).astype(BF16))
            up = jnp.maximum(_dot(hbs[p], w_up_ref[:, cols]), 0.0)
            part = _dot((up * up).astype(BF16), w_down_ref[cols, :])
            yfs[p] = part if yfs[p] is None else yfs[p] + part
    for p, rows in enumerate(parts):
        o_ref[rows, :] = x_ref[rows, :] + _rms(yfs[p], g_ref[5:6, :])


def _ffn(layer, x, g_norm, w_up, w_down, to_cast, cast_layer):
    t, d = x.shape
    tb = WIDE_BLOCK
    n_steps = t // tb
    cast_in, cast_out, cast_shapes = _cast_plan(to_cast, cast_layer, n_steps)
    x_spec = pl.BlockSpec((tb, d), lambda i: (i, 0))
    outs = pl.pallas_call(
        functools.partial(_ffn_kernel, len(to_cast)),
        out_shape=[jax.ShapeDtypeStruct((t, d), F32)] + cast_shapes,
        grid=(n_steps,),
        in_specs=[x_spec, _layer_spec(g_norm, layer), _whole_spec(w_up), _whole_spec(w_down)] + cast_in,
        out_specs=[x_spec] + cast_out,
        compiler_params=pltpu.CompilerParams(
            dimension_semantics=("arbitrary",), vmem_limit_bytes=VMEM_LIMIT_BYTES),
        name="ffn",
    )(x, g_norm, w_up, w_down, *to_cast)
    return outs[0], outs[1:]


def kernel(x, mem, g_norm, g_mem, w_in, attn_sinks, w_spatial, b_spatial, g_sgu, w_pool, pool_scale,
           w_branch, w_out, w_q_mem, w_kv_mem, w_o_mem, w_up, w_down):
    b, s, d = x.shape
    assert d == D_MODEL and s % TOKEN_BLOCK == 0 and TOKEN_BLOCK % ATT_BLOCK == 0 and s % WIDE_BLOCK == 0
    assert s % MEM_BLOCK == 0
    xt = x.reshape(b * s, d)

    b_sp_full = jnp.repeat(jnp.swapaxes(b_spatial, 1, 2), SGU_GROUP_DIM, axis=2)
    g_sgu3 = g_sgu.reshape(DEPTH, 1, SGU_DIM)
    pool_scale3 = pool_scale.reshape(DEPTH, 1, POOL_DIM)
    mixer_w = (w_in, w_pool.reshape(DEPTH, POOL_DIM, POOL_GROUP_DIM),
               w_branch.reshape(DEPTH, N_BRANCH * BRANCH_DIM, D_MODEL), w_out)
    later_w = (w_q_mem, w_o_mem, w_up, w_down)

    k_mem, v_mem, mixer_b = _mem_kv(mem, g_mem.reshape(DEPTH, 1, D_MODEL), w_kv_mem, mixer_w)
    for l in range(DEPTH):
        w_in_b, w_pool_b, w_branch_b, w_out_b = mixer_b
        xt, (w_q_b, w_o_b, w_up_b, w_down_b) = _mixer(
            l, xt, attn_sinks[l], g_norm, w_in_b, w_spatial, b_sp_full, g_sgu3, w_pool_b, pool_scale3,
            w_branch_b, w_out_b, later_w, s)
        xt = _mem_attn(l, xt, g_norm, w_q_b, k_mem, v_mem, w_o_b, s)
        xt, mixer_b = _ffn(l, xt, g_norm, w_up_b, w_down_b, mixer_w if l + 1 < DEPTH else (), l + 1)
    return xt.reshape(b, s, d)
```

```python
import functools
import math

import jax
import jax.numpy as jnp
from jax import lax
from jax.experimental import pallas as pl
from jax.experimental.pallas import tpu as pltpu

D_MODEL = 1024
DEPTH = 4
CHUNK = 64
N_MEM = 256
EPS = 1e-6
NEG_INF = -1e30
LOG2_E = math.log2(math.e)
N_NORMS = 6

BRANCH_DIM = D_MODEL // 2
N_BRANCH = 3
HEAD_DIM = 64
A_Q_HEADS = BRANCH_DIM // HEAD_DIM
A_KV_HEADS = 2
A_GROUP = A_Q_HEADS // A_KV_HEADS
A_Q_DIM = A_Q_HEADS * HEAD_DIM
A_KV_DIM = A_KV_HEADS * HEAD_DIM
ATT_BLOCK = 128
SGU_CHUNK = 128
SGU_GROUPS = 4
SGU_DIM = BRANCH_DIM
SGU_GROUP_DIM = SGU_DIM // SGU_GROUPS
POOL_WINDOWS = (2, 4, 8, 16)
POOL_GROUPS = 4
POOL_DIM = BRANCH_DIM
POOL_GROUP_DIM = POOL_DIM // POOL_GROUPS
POOL_HALO = 16
IN_DIM = A_Q_DIM + 2 * A_KV_DIM + 2 * SGU_DIM + POOL_DIM + N_BRANCH * D_MODEL
MEM_HEADS = 4
MEM_HEAD_DIM = 128
MEM_DIM = MEM_HEADS * MEM_HEAD_DIM
D_FF = 4 * D_MODEL

OFF_Q = 0
OFF_K = OFF_Q + A_Q_DIM
OFF_V = OFF_K + A_KV_DIM
OFF_SU = OFF_V + A_KV_DIM
OFF_SV = OFF_SU + SGU_DIM
OFF_PC = OFF_SV + SGU_DIM
OFF_GATE = OFF_PC + POOL_DIM

LANES = 128
BF16_SUBLANES = 16
TOKEN_BLOCK = 512
GATE_COLS = 256
WIDE_BLOCK = 1024
FF_CHUNK = 1024
FF_PARTS = 4
VMEM_LIMIT_BYTES = 56 * 1024 * 1024
N_MIXER_IN = 15

BF16 = jnp.bfloat16
F32 = jnp.float32


def _rms(x, g):
    ms = jnp.mean(x * x, axis=-1, keepdims=True)
    return x * lax.rsqrt(ms + EPS) * g


def _gelu(x):
    c = math.sqrt(2.0 / math.pi)
    t = jnp.tanh(x * (c + (c * 0.044715) * (x * x)))
    hx = 0.5 * x
    return hx + hx * t


def _sigmoid(x):
    return 0.5 * jnp.tanh(0.5 * x) + 0.5


def _dot(a, b):
    return jnp.dot(a, b, preferred_element_type=F32)


def _dot_nt(a, b):
    return lax.dot_general(a, b, (((1,), (1,)), ((), ())), preferred_element_type=F32)


def _layer_spec(stacked, layer):
    tail = (0,) * (stacked.ndim - 1)
    return pl.BlockSpec((None,) + stacked.shape[1:], lambda *_: (layer,) + tail,
                        pipeline_mode=pl.Buffered(1))


def _whole_spec(a):
    zeros = (0,) * a.ndim
    return pl.BlockSpec(a.shape, lambda *_: zeros, pipeline_mode=pl.Buffered(1))


def _cast_plan(stacked_f32, layer, n_steps, step_of=lambda i: i):
    in_specs, out_specs, out_shapes = [], [], []
    for a in stacked_f32:
        r, c = a.shape[1:]
        rb = r // n_steps
        assert rb * n_steps == r and rb % BF16_SUBLANES == 0
        in_specs.append(pl.BlockSpec((None, rb, c), lambda *g: (layer, step_of(*g), 0)))
        out_specs.append(pl.BlockSpec((rb, c), lambda *g: (step_of(*g), 0)))
        out_shapes.append(jax.ShapeDtypeStruct((r, c), BF16))
    return in_specs, out_specs, out_shapes


def _cast_rows(srcs, dsts):
    for src, dst in zip(srcs, dsts):
        dst[...] = src[...].astype(BF16)


def _mixer_kernel(blocks_per_seq, n_cast, *refs):
    (sinks_ref, x_ref, g_ref, w_in_ref, w_sp_ref, b_sp_ref, g_sgu_ref, w_pool_ref, pool_scale_ref,
     w_branch_ref, w_out_ref, w_q_ref, k_ref, v_ref, w_o_ref) = refs[:N_MIXER_IN]
    cast_src = refs[N_MIXER_IN:N_MIXER_IN + n_cast]
    o_ref = refs[N_MIXER_IN + n_cast]
    cast_dst = refs[N_MIXER_IN + n_cast + 1:N_MIXER_IN + 2 * n_cast + 1]
    kbuf, vbuf, cbuf, ya_ref, yb_ref, yc_ref, om_ref = refs[N_MIXER_IN + 2 * n_cast + 1:]
    tb = x_ref.shape[0]
    n_qblk = tb // ATT_BLOCK
    step = pl.program_id(0)
    first = (step % blocks_per_seq) == 0
    seq_pos0 = (step % blocks_per_seq) * tb
    half = tb // 2
    blocks_per_gate = n_qblk // (D_MODEL // GATE_COLS)

    @pl.when(first)
    def _():
        kbuf[0:ATT_BLOCK, :] = jnp.zeros((ATT_BLOCK, A_KV_DIM), F32)
        vbuf[0:ATT_BLOCK, :] = jnp.zeros((ATT_BLOCK, A_KV_DIM), F32)
        cbuf[0:POOL_HALO, :] = jnp.zeros((POOL_HALO, POOL_DIM), F32)

    hb_halves = [_rms(x_ref[r * half:(r + 1) * half, :], g_ref[0:1, :]).astype(BF16) for r in range(2)]
    qkv = jnp.concatenate([_dot(hb_r, w_in_ref[:, OFF_Q:OFF_SU]) for hb_r in hb_halves], axis=0)
    hb = jnp.concatenate(hb_halves, axis=0)

    q = qkv[:, 0:A_Q_DIM] * (LOG2_E / math.sqrt(HEAD_DIM))
    k_new = qkv[:, A_Q_DIM:A_Q_DIM + A_KV_DIM]
    v_new = qkv[:, A_Q_DIM + A_KV_DIM:A_Q_DIM + 2 * A_KV_DIM]
    kbuf[ATT_BLOCK:ATT_BLOCK + tb, :] = k_new
    vbuf[ATT_BLOCK:ATT_BLOCK + tb, :] = v_new
    k_all = kbuf[...]
    v_all = vbuf[...]
    k_sw = pltpu.roll(k_all, HEAD_DIM, 1)
    lo_kv = lax.broadcasted_iota(jnp.int32, k_all.shape, 1) < HEAD_DIM
    k_dup = (jnp.where(lo_kv, k_all, k_sw).astype(BF16), jnp.where(lo_kv, k_sw, k_all).astype(BF16))
    v_t = v_all.T.astype(BF16)
    ones_rows = jnp.ones((BF16_SUBLANES, 2 * ATT_BLOCK), BF16)
    kbuf[0:ATT_BLOCK, :] = k_new[tb - ATT_BLOCK:tb, :]
    vbuf[0:ATT_BLOCK, :] = v_new[tb - ATT_BLOCK:tb, :]

    lo_q = lax.broadcasted_iota(jnp.int32, (ATT_BLOCK, LANES), 1) < HEAD_DIM
    k_row = lax.broadcasted_iota(jnp.int32, (2 * ATT_BLOCK, ATT_BLOCK), 0)
    k_chunk = k_row // CHUNK
    q_chunk = lax.broadcasted_iota(jnp.int32, (2 * ATT_BLOCK, ATT_BLOCK), 1) // CHUNK
    band = (k_chunk >= q_chunk) & (k_chunk <= q_chunk + 2)
    k_min = jnp.where(first, ATT_BLOCK, 0)

    def scores(j):
        rows = slice(j * ATT_BLOCK, (j + 1) * ATT_BLOCK)
        keys = slice(j * ATT_BLOCK, (j + 2) * ATT_BLOCK)
        out = []
        for h in range(A_KV_HEADS):
            q_stack = []
            for p in (2 * h, 2 * h + 1):
                qp = q[rows, p * LANES:(p + 1) * LANES]
                q_stack.append(jnp.where(lo_q, qp, 0.0).astype(BF16))
                q_stack.append(jnp.where(lo_q, 0.0, qp).astype(BF16))
            out.append(_dot_nt(k_dup[h][keys], jnp.concatenate(q_stack, axis=0)))
        return out

    def attend(j, s_heads):
        rows = slice(j * ATT_BLOCK, (j + 1) * ATT_BLOCK)
        keys = slice(j * ATT_BLOCK, (j + 2) * ATT_BLOCK)
        valid = (band & (k_row >= k_min)) if j == 0 else band
        for h in range(A_KV_HEADS):
            probs, sink_w = [], []
            for a in range(A_GROUP):
                sink = sinks_ref[h * A_GROUP + a] * LOG2_E
                s = jnp.where(valid, s_heads[h][:, a * ATT_BLOCK:(a + 1) * ATT_BLOCK], NEG_INF)
                m = jnp.maximum(jnp.max(s, axis=0, keepdims=True), sink)
                probs.append(jnp.exp2(s - m).astype(BF16))
                sink_w.append(jnp.exp2(sink - m))
            v_aug = jnp.concatenate([v_t[h * HEAD_DIM:(h + 1) * HEAD_DIM, keys], ones_rows], axis=0)
            o_aug = _dot(v_aug, jnp.concatenate(probs, axis=1))
            denom = o_aug[HEAD_DIM:HEAD_DIM + 1, :] + jnp.concatenate(sink_w, axis=1)
            o_t = o_aug[0:HEAD_DIM, :] * (1.0 / denom)
            for pi in range(2):
                pair_t = jnp.concatenate(
                    [o_t[:, (2 * pi + i) * ATT_BLOCK:(2 * pi + i + 1) * ATT_BLOCK] for i in range(2)], axis=0)
                p = 2 * h + pi
                ya_ref[rows, p * LANES:(p + 1) * LANES] = pair_t.T.astype(BF16)

    def gate0_chunk(c):
        c0 = OFF_GATE + c * GATE_COLS
        return _dot(hb, w_in_ref[:, c0:c0 + GATE_COLS])

    s_cur = scores(0)
    gate0 = [gate0_chunk(0)]
    for j in range(n_qblk):
        s_next = scores(j + 1) if j + 1 < n_qblk else None
        if (j + 1) % blocks_per_gate == 0 and (j + 1) // blocks_per_gate < D_MODEL // GATE_COLS:
            gate0.append(gate0_chunk((j + 1) // blocks_per_gate))
        attend(j, s_cur)
        s_cur = s_next
    gates = [_sigmoid(jnp.concatenate(gate0, axis=1))]

    _cast_rows(cast_src, cast_dst)

    uv = _dot(hb, w_in_ref[:, OFF_SU:OFF_PC])
    pc = _dot(hb, w_in_ref[:, OFF_PC:OFF_GATE])
    gates.append(_sigmoid(_dot(hb, w_in_ref[:, OFF_GATE + D_MODEL:OFF_GATE + 2 * D_MODEL])))
    u = _gelu(uv[:, 0:SGU_DIM])
    vn = _rms(_gelu(uv[:, SGU_DIM:2 * SGU_DIM]), g_sgu_ref[...]).astype(BF16)
    sp_row = lax.broadcasted_iota(jnp.int32, (SGU_CHUNK, SGU_CHUNK), 0) // CHUNK
    sp_col = lax.broadcasted_iota(jnp.int32, (SGU_CHUNK, SGU_CHUNK), 1) // CHUNK
    w_sp = [jnp.where(sp_col <= sp_row, w_sp_ref[g], 0.0).astype(BF16) for g in range(SGU_GROUPS)]
    for c in range(tb // SGU_CHUNK):
        rows = slice(c * SGU_CHUNK, (c + 1) * SGU_CHUNK)
        for g in range(SGU_GROUPS):
            cols = slice(g * SGU_GROUP_DIM, (g + 1) * SGU_GROUP_DIM)
            sp = _dot(w_sp[g], vn[rows, cols]) + b_sp_ref[:, cols]
            yb_ref[rows, cols] = (u[rows, cols] * sp).astype(BF16)

    gates.append(_sigmoid(_dot(hb, w_in_ref[:, OFF_GATE + 2 * D_MODEL:OFF_GATE + 3 * D_MODEL])))
    cbuf[POOL_HALO:POOL_HALO + tb, :] = pc
    t_pos = seq_pos0 + lax.broadcasted_iota(jnp.int32, (tb, POOL_GROUP_DIM), 0)
    for g, w in enumerate(POOL_WINDOWS):
        cols = slice(g * POOL_GROUP_DIM, (g + 1) * POOL_GROUP_DIM)
        acc = cbuf[:, cols]
        lag = 1
        while lag < w:
            acc = acc + pltpu.roll(acc, lag, 0)
            lag *= 2
        cnt = jnp.minimum(t_pos + 1, w).astype(F32)
        pooled = (acc[POOL_HALO:, :] / cnt - pc[:, cols]).astype(BF16)
        mixed = _dot(pooled, w_pool_ref[cols, :]) * pool_scale_ref[:, cols]
        yc_ref[:, cols] = mixed.astype(BF16)
    cbuf[0:POOL_HALO, :] = pc[tb - POOL_HALO:tb, :]

    merged = []
    for r in range(2):
        rows = slice(r * half, (r + 1) * half)
        acc = None
        for n, y_ref in enumerate((ya_ref, yb_ref, yc_ref)):
            proj = _dot(y_ref[rows, :], w_branch_ref[n * BRANCH_DIM:(n + 1) * BRANCH_DIM, :])
            term = gates[n][rows] * proj
            acc = term if acc is None else acc + term
        merged.append(acc.astype(BF16))
    merged = jnp.concatenate(merged, axis=0)
    halves = [slice(r * half, (r + 1) * half) for r in range(2)]
    x_mid = [x_ref[rows, :] + _rms(_dot(merged[rows], w_out_ref[...]), g_ref[1:2, :]) for rows in halves]

    ones = jnp.ones((N_MEM, MEM_HEAD_DIM), BF16)
    v_aug = [jnp.concatenate([v_ref[:, h * MEM_HEAD_DIM:(h + 1) * MEM_HEAD_DIM], ones], axis=1)
             for h in range(MEM_HEADS)]

    def mem_scores(x1):
        hm = _rms(x1, g_ref[2:3, :]).astype(BF16)
        qm = (_dot(hm, w_q_ref[...]) * (LOG2_E / math.sqrt(MEM_HEAD_DIM))).astype(BF16)
        return [_dot_nt(qm[:, h * MEM_HEAD_DIM:(h + 1) * MEM_HEAD_DIM],
                        k_ref[:, h * MEM_HEAD_DIM:(h + 1) * MEM_HEAD_DIM]) for h in range(MEM_HEADS)]

    def mem_attend(rows, x1, s_heads):
        for h, s in enumerate(s_heads):
            cols = slice(h * MEM_HEAD_DIM, (h + 1) * MEM_HEAD_DIM)
            m = jnp.max(s, axis=-1, keepdims=True)
            o_aug = _dot(jnp.exp2(s - m).astype(BF16), v_aug[h])
            om_ref[rows, cols] = (o_aug[:, :MEM_HEAD_DIM] / o_aug[:, MEM_HEAD_DIM:]).astype(BF16)
        ym = _dot(om_ref[rows, :], w_o_ref[...])
        o_ref[rows, :] = x1 + _rms(ym, g_ref[3:4, :])

    s_mem = [mem_scores(x1) for x1 in x_mid]
    for rows, x1, s_heads in zip(halves, x_mid, s_mem):
        mem_attend(rows, x1, s_heads)


def _mixer(layer, x, sinks, g_norm, w_in, w_sp, b_sp_full, g_sgu, w_pool, pool_scale, w_branch, w_out,
           w_q, k_mem, v_mem, w_o, to_cast, seq_len):
    t, d = x.shape
    tb = TOKEN_BLOCK
    assert (tb // ATT_BLOCK) % (D_MODEL // GATE_COLS) == 0 and tb % 2 == 0
    n_steps = t // tb
    bps = seq_len // tb
    kv_spec = pl.BlockSpec((None, None, N_MEM, MEM_DIM), lambda i: (layer, i // bps, 0, 0))
    kern = functools.partial(_mixer_kernel, seq_len // tb, len(to_cast))
    cast_in, cast_out, cast_shapes = _cast_plan(to_cast, layer, n_steps)
    x_spec = pl.BlockSpec((tb, d), lambda i: (i, 0))
    outs = pl.pallas_call(
        kern,
        out_shape=[jax.ShapeDtypeStruct((t, d), F32)] + cast_shapes,
        grid=(n_steps,),
        in_specs=[pl.BlockSpec(memory_space=pltpu.SMEM), x_spec, _layer_spec(g_norm, layer),
                  _whole_spec(w_in), _layer_spec(w_sp, layer), _layer_spec(b_sp_full, layer),
                  _layer_spec(g_sgu, layer), _whole_spec(w_pool), _layer_spec(pool_scale, layer),
                  _whole_spec(w_branch), _whole_spec(w_out), _whole_spec(w_q), kv_spec, kv_spec,
                  _whole_spec(w_o)] + cast_in,
        out_specs=[x_spec] + cast_out,
        scratch_shapes=[
            pltpu.VMEM((ATT_BLOCK + tb, A_KV_DIM), F32),
            pltpu.VMEM((ATT_BLOCK + tb, A_KV_DIM), F32),
            pltpu.VMEM((POOL_HALO + tb, POOL_DIM), F32),
            pltpu.VMEM((tb, BRANCH_DIM), BF16),
            pltpu.VMEM((tb, BRANCH_DIM), BF16),
            pltpu.VMEM((tb, BRANCH_DIM), BF16),
            pltpu.VMEM((tb, MEM_DIM), BF16),
        ],
        compiler_params=pltpu.CompilerParams(
            dimension_semantics=("arbitrary",), vmem_limit_bytes=VMEM_LIMIT_BYTES),
        name="mixer",
    )(sinks, x, g_norm, w_in, w_sp, b_sp_full, g_sgu, w_pool, pool_scale, w_branch, w_out,
      w_q, k_mem, v_mem, w_o, *to_cast)
    return outs[0], outs[1:]


def _mem_kv_kernel(n_cast, mem_ref, g_ref, w_ref, *refs):
    cast_src = refs[:n_cast]
    k_ref, v_ref = refs[n_cast:n_cast + 2]
    cast_dst = refs[n_cast + 2:2 * n_cast + 2]
    w_bf = refs[2 * n_cast + 2]
    _cast_rows(cast_src, cast_dst)

    @pl.when(pl.program_id(1) == 0)
    def _():
        w_bf[...] = w_ref[...].astype(BF16)

    mem_n = _rms(mem_ref[...], g_ref[...]).astype(BF16)
    kv = _dot(mem_n, w_bf[...])
    k_ref[...] = kv[:, 0:MEM_DIM].astype(BF16)
    v_ref[...] = kv[:, MEM_DIM:2 * MEM_DIM].astype(BF16)


def _mem_kv(mem, g_mem, w_kv, to_cast):
    b = mem.shape[0]
    out = jax.ShapeDtypeStruct((DEPTH, b, N_MEM, MEM_DIM), BF16)
    kv_spec = pl.BlockSpec((None, None, N_MEM, MEM_DIM), lambda l, i: (l, i, 0, 0))
    cast_in, cast_out, cast_shapes = _cast_plan(to_cast, 0, DEPTH * b, lambda l, i: l * b + i)
    outs = pl.pallas_call(
        functools.partial(_mem_kv_kernel, len(to_cast)),
        out_shape=[out, out] + cast_shapes,
        grid=(DEPTH, b),
        in_specs=[
            pl.BlockSpec((None, N_MEM, D_MODEL), lambda l, i: (i, 0, 0)),
            pl.BlockSpec((None, 1, D_MODEL), lambda l, i: (l, 0, 0)),
            pl.BlockSpec((None, D_MODEL, 2 * MEM_DIM), lambda l, i: (l, 0, 0)),
        ] + cast_in,
        out_specs=[kv_spec, kv_spec] + cast_out,
        scratch_shapes=[pltpu.VMEM((D_MODEL, 2 * MEM_DIM), BF16)],
        compiler_params=pltpu.CompilerParams(
            dimension_semantics=("arbitrary", "arbitrary"), vmem_limit_bytes=VMEM_LIMIT_BYTES),
        name="mem_kv",
    )(mem, g_mem, w_kv, *to_cast)
    return outs[0], outs[1], outs[2:]


def _ffn_kernel(n_cast, x_ref, g_ref, w_up_ref, w_down_ref, *refs):
    cast_src = refs[:n_cast]
    o_ref = refs[n_cast]
    cast_dst = refs[n_cast + 1:]
    rows_per_part = x_ref.shape[0] // FF_PARTS
    parts = [slice(p * rows_per_part, (p + 1) * rows_per_part) for p in range(FF_PARTS)]
    hbs, yfs = [], [None] * FF_PARTS
    for c in range(D_FF // FF_CHUNK):
        cols = slice(c * FF_CHUNK, (c + 1) * FF_CHUNK)
        if c == 1:
            _cast_rows(cast_src, cast_dst)
        for p, rows in enumerate(parts):
            if c == 0:
                hbs.append(_rms(x_ref[rows, :], g_ref[4:5, :]).astype(BF16))
            up = jnp.maximum(_dot(hbs[p], w_up_ref[:, cols]), 0.0)
            part = _dot((up * up).astype(BF16), w_down_ref[cols, :])
            yfs[p] = part if yfs[p] is None else yfs[p] + part
    for p, rows in enumerate(parts):
        o_ref[rows, :] = x_ref[rows, :] + _rms(yfs[p], g_ref[5:6, :])


def _ffn(layer, x, g_norm, w_up, w_down, to_cast, cast_layer):
    t, d = x.shape
    tb = WIDE_BLOCK
    n_steps = t // tb
    cast_in, cast_out, cast_shapes = _cast_plan(to_cast, cast_layer, n_steps)
    x_spec = pl.BlockSpec((tb, d), lambda i: (i, 0))
    outs = pl.pallas_call(
        functools.partial(_ffn_kernel, len(to_cast)),
        out_shape=[jax.ShapeDtypeStruct((t, d), F32)] + cast_shapes,
        grid=(n_steps,),
        in_specs=[x_spec, _layer_spec(g_norm, layer), _whole_spec(w_up), _whole_spec(w_down)] + cast_in,
        out_specs=[x_spec] + cast_out,
        compiler_params=pltpu.CompilerParams(
            dimension_semantics=("arbitrary",), vmem_limit_bytes=VMEM_LIMIT_BYTES),
        name="ffn",
    )(x, g_norm, w_up, w_down, *to_cast)
    return outs[0], outs[1:]


def kernel(x, mem, g_norm, g_mem, w_in, attn_sinks, w_spatial, b_spatial, g_sgu, w_pool, pool_scale,
           w_branch, w_out, w_q_mem, w_kv_mem, w_o_mem, w_up, w_down):
    b, s, d = x.shape
    assert d == D_MODEL and s % TOKEN_BLOCK == 0 and TOKEN_BLOCK % ATT_BLOCK == 0 and s % WIDE_BLOCK == 0
    xt = x.reshape(b * s, d)

    b_sp_full = jnp.repeat(jnp.swapaxes(b_spatial, 1, 2), SGU_GROUP_DIM, axis=2)
    g_sgu3 = g_sgu.reshape(DEPTH, 1, SGU_DIM)
    pool_scale3 = pool_scale.reshape(DEPTH, 1, POOL_DIM)
    mixer_w = (w_in, w_pool.reshape(DEPTH, POOL_DIM, POOL_GROUP_DIM),
               w_branch.reshape(DEPTH, N_BRANCH * BRANCH_DIM, D_MODEL), w_out, w_q_mem, w_o_mem)
    ffn_w = (w_up, w_down)

    k_mem, v_mem, mixer_b = _mem_kv(mem, g_mem.reshape(DEPTH, 1, D_MODEL), w_kv_mem, mixer_w)
    for l in range(DEPTH):
        w_in_b, w_pool_b, w_branch_b, w_out_b, w_q_b, w_o_b = mixer_b
        xt, (w_up_b, w_down_b) = _mixer(
            l, xt, attn_sinks[l], g_norm, w_in_b, w_spatial, b_sp_full, g_sgu3, w_pool_b, pool_scale3,
            w_branch_b, w_out_b, w_q_b, k_mem, v_mem, w_o_b, ffn_w, s)
        xt, mixer_b = _ffn(l, xt, g_norm, w_up_b, w_down_b, mixer_w if l + 1 < DEPTH else (), l + 1)
    return xt.reshape(b, s, d)
```

```python
import functools
import math

import jax
import jax.numpy as jnp
from jax import lax
from jax.experimental import pallas as pl
from jax.experimental.pallas import tpu as pltpu

D_MODEL = 1024
DEPTH = 4
CHUNK = 64
N_MEM = 256
EPS = 1e-6
NEG_INF = -1e30
LOG2_E = math.log2(math.e)
N_NORMS = 6

BRANCH_DIM = D_MODEL // 2
N_BRANCH = 3
HEAD_DIM = 64
A_Q_HEADS = BRANCH_DIM // HEAD_DIM
A_KV_HEADS = 2
A_GROUP = A_Q_HEADS // A_KV_HEADS
A_Q_DIM = A_Q_HEADS * HEAD_DIM
A_KV_DIM = A_KV_HEADS * HEAD_DIM
ATT_BLOCK = 128
SGU_CHUNK = 128
SGU_GROUPS = 4
SGU_DIM = BRANCH_DIM
SGU_GROUP_DIM = SGU_DIM // SGU_GROUPS
POOL_WINDOWS = (2, 4, 8, 16)
POOL_GROUPS = 4
POOL_DIM = BRANCH_DIM
POOL_GROUP_DIM = POOL_DIM // POOL_GROUPS
POOL_HALO = 16
IN_DIM = A_Q_DIM + 2 * A_KV_DIM + 2 * SGU_DIM + POOL_DIM + N_BRANCH * D_MODEL
MEM_HEADS = 4
MEM_HEAD_DIM = 128
MEM_DIM = MEM_HEADS * MEM_HEAD_DIM
D_FF = 4 * D_MODEL

OFF_Q = 0
OFF_K = OFF_Q + A_Q_DIM
OFF_V = OFF_K + A_KV_DIM
OFF_SU = OFF_V + A_KV_DIM
OFF_SV = OFF_SU + SGU_DIM
OFF_PC = OFF_SV + SGU_DIM
OFF_GATE = OFF_PC + POOL_DIM

LANES = 128
BF16_SUBLANES = 16
TOKEN_BLOCK = 512
GATE_COLS = 256
WIDE_BLOCK = 1024
FF_CHUNK = 1024
FF_PARTS = 4
VMEM_LIMIT_BYTES = 56 * 1024 * 1024
N_MIXER_IN = 16

BF16 = jnp.bfloat16
F32 = jnp.float32


def _rms(x, g):
    ms = jnp.mean(x * x, axis=-1, keepdims=True)
    return x * lax.rsqrt(ms + EPS) * g


def _gelu(x):
    c = math.sqrt(2.0 / math.pi)
    t = jnp.tanh(x * (c + (c * 0.044715) * (x * x)))
    hx = 0.5 * x
    return hx + hx * t


def _sigmoid(x):
    return 0.5 * jnp.tanh(0.5 * x) + 0.5


def _dot(a, b):
    return jnp.dot(a, b, preferred_element_type=F32)


def _dot_nt(a, b):
    return lax.dot_general(a, b, (((1,), (1,)), ((), ())), preferred_element_type=F32)


def _layer_spec(stacked, layer):
    tail = (0,) * (stacked.ndim - 1)
    return pl.BlockSpec((None,) + stacked.shape[1:], lambda *_: (layer,) + tail,
                        pipeline_mode=pl.Buffered(1))


def _whole_spec(a):
    zeros = (0,) * a.ndim
    return pl.BlockSpec(a.shape, lambda *_: zeros, pipeline_mode=pl.Buffered(1))


def _cast_plan(stacked_f32, layer, n_steps, step_of=lambda i: i):
    in_specs, out_specs, out_shapes = [], [], []
    for a in stacked_f32:
        r, c = a.shape[1:]
        rb = r // n_steps
        assert rb * n_steps == r and rb % BF16_SUBLANES == 0
        in_specs.append(pl.BlockSpec((None, rb, c), lambda *g: (layer, step_of(*g), 0)))
        out_specs.append(pl.BlockSpec((rb, c), lambda *g: (step_of(*g), 0)))
        out_shapes.append(jax.ShapeDtypeStruct((r, c), BF16))
    return in_specs, out_specs, out_shapes


def _cast_rows(srcs, dsts):
    for src, dst in zip(srcs, dsts):
        dst[...] = src[...].astype(BF16)


def _mixer_kernel(blocks_per_seq, n_cast, *refs):
    (sinks_ref, x_ref, g_ref, w_in_ref, w_sp_ref, b_sp_ref, g_sgu_ref, w_pool_ref, pool_scale_ref,
     w_branch_ref, w_out_ref, w_q_ref, k_ref, v_ref, w_o_ref, xn_ref) = refs[:N_MIXER_IN]
    cast_src = refs[N_MIXER_IN:N_MIXER_IN + n_cast]
    o_ref = refs[N_MIXER_IN + n_cast]
    cast_dst = refs[N_MIXER_IN + n_cast + 1:N_MIXER_IN + 2 * n_cast + 1]
    kbuf, vbuf, cbuf, ya_ref, yb_ref, yc_ref, om_ref, hb_ref, qkv_ref = refs[N_MIXER_IN + 2 * n_cast + 1:]
    tb = x_ref.shape[0]
    n_qblk = tb // ATT_BLOCK
    step = pl.program_id(0)
    first = (step % blocks_per_seq) == 0
    seq_pos0 = (step % blocks_per_seq) * tb
    half = tb // 2
    blocks_per_gate = n_qblk // (D_MODEL // GATE_COLS)

    def norm_and_qkv(src_ref, rows):
        hb_r = _rms(src_ref[rows, :], g_ref[0:1, :]).astype(BF16)
        hb_ref[rows, :] = hb_r
        qkv_ref[rows, :] = _dot(hb_r, w_in_ref[:, OFF_Q:OFF_SU])

    @pl.when(step == 0)
    def _():
        for r in range(2):
            norm_and_qkv(x_ref, slice(r * half, (r + 1) * half))

    @pl.when(first)
    def _():
        kbuf[0:ATT_BLOCK, :] = jnp.zeros((ATT_BLOCK, A_KV_DIM), F32)
        vbuf[0:ATT_BLOCK, :] = jnp.zeros((ATT_BLOCK, A_KV_DIM), F32)
        cbuf[0:POOL_HALO, :] = jnp.zeros((POOL_HALO, POOL_DIM), F32)

    hb = hb_ref[...]
    qkv = qkv_ref[...]

    q = qkv[:, 0:A_Q_DIM] * (LOG2_E / math.sqrt(HEAD_DIM))
    k_new = qkv[:, A_Q_DIM:A_Q_DIM + A_KV_DIM]
    v_new = qkv[:, A_Q_DIM + A_KV_DIM:A_Q_DIM + 2 * A_KV_DIM]
    kbuf[ATT_BLOCK:ATT_BLOCK + tb, :] = k_new
    vbuf[ATT_BLOCK:ATT_BLOCK + tb, :] = v_new
    k_all = kbuf[...]
    v_all = vbuf[...]
    k_sw = pltpu.roll(k_all, HEAD_DIM, 1)
    lo_kv = lax.broadcasted_iota(jnp.int32, k_all.shape, 1) < HEAD_DIM
    k_dup = (jnp.where(lo_kv, k_all, k_sw).astype(BF16), jnp.where(lo_kv, k_sw, k_all).astype(BF16))
    v_t = v_all.T.astype(BF16)
    ones_rows = jnp.ones((BF16_SUBLANES, 2 * ATT_BLOCK), BF16)
    kbuf[0:ATT_BLOCK, :] = k_new[tb - ATT_BLOCK:tb, :]
    vbuf[0:ATT_BLOCK, :] = v_new[tb - ATT_BLOCK:tb, :]

    lo_q = lax.broadcasted_iota(jnp.int32, (ATT_BLOCK, LANES), 1) < HEAD_DIM
    k_row = lax.broadcasted_iota(jnp.int32, (2 * ATT_BLOCK, ATT_BLOCK), 0)
    k_chunk = k_row // CHUNK
    q_chunk = lax.broadcasted_iota(jnp.int32, (2 * ATT_BLOCK, ATT_BLOCK), 1) // CHUNK
    band = (k_chunk >= q_chunk) & (k_chunk <= q_chunk + 2)
    k_min = jnp.where(first, ATT_BLOCK, 0)

    def scores(j):
        rows = slice(j * ATT_BLOCK, (j + 1) * ATT_BLOCK)
        keys = slice(j * ATT_BLOCK, (j + 2) * ATT_BLOCK)
        out = []
        for h in range(A_KV_HEADS):
            q_stack = []
            for p in (2 * h, 2 * h + 1):
                qp = q[rows, p * LANES:(p + 1) * LANES]
                q_stack.append(jnp.where(lo_q, qp, 0.0).astype(BF16))
                q_stack.append(jnp.where(lo_q, 0.0, qp).astype(BF16))
            out.append(_dot_nt(k_dup[h][keys], jnp.concatenate(q_stack, axis=0)))
        return out

    def attend(j, s_heads):
        rows = slice(j * ATT_BLOCK, (j + 1) * ATT_BLOCK)
        keys = slice(j * ATT_BLOCK, (j + 2) * ATT_BLOCK)
        valid = (band & (k_row >= k_min)) if j == 0 else band
        for h in range(A_KV_HEADS):
            probs, sink_w = [], []
            for a in range(A_GROUP):
                sink = sinks_ref[h * A_GROUP + a] * LOG2_E
                s = jnp.where(valid, s_heads[h][:, a * ATT_BLOCK:(a + 1) * ATT_BLOCK], NEG_INF)
                m = jnp.maximum(jnp.max(s, axis=0, keepdims=True), sink)
                probs.append(jnp.exp2(s - m).astype(BF16))
                sink_w.append(jnp.exp2(sink - m))
            v_aug = jnp.concatenate([v_t[h * HEAD_DIM:(h + 1) * HEAD_DIM, keys], ones_rows], axis=0)
            o_aug = _dot(v_aug, jnp.concatenate(probs, axis=1))
            denom = o_aug[HEAD_DIM:HEAD_DIM + 1, :] + jnp.concatenate(sink_w, axis=1)
            o_t = o_aug[0:HEAD_DIM, :] * (1.0 / denom)
            for pi in range(2):
                pair_t = jnp.concatenate(
                    [o_t[:, (2 * pi + i) * ATT_BLOCK:(2 * pi + i + 1) * ATT_BLOCK] for i in range(2)], axis=0)
                p = 2 * h + pi
                ya_ref[rows, p * LANES:(p + 1) * LANES] = pair_t.T.astype(BF16)

    def gate0_chunk(c):
        c0 = OFF_GATE + c * GATE_COLS
        return _dot(hb, w_in_ref[:, c0:c0 + GATE_COLS])

    s_cur = scores(0)
    gate0 = [gate0_chunk(0)]
    for j in range(n_qblk):
        s_next = scores(j + 1) if j + 1 < n_qblk else None
        if (j + 1) % blocks_per_gate == 0 and (j + 1) // blocks_per_gate < D_MODEL // GATE_COLS:
            gate0.append(gate0_chunk((j + 1) // blocks_per_gate))
        attend(j, s_cur)
        s_cur = s_next
    gates = [_sigmoid(jnp.concatenate(gate0, axis=1))]

    _cast_rows(cast_src, cast_dst)

    uv = _dot(hb, w_in_ref[:, OFF_SU:OFF_PC])
    pc = _dot(hb, w_in_ref[:, OFF_PC:OFF_GATE])
    gates.append(_sigmoid(_dot(hb, w_in_ref[:, OFF_GATE + D_MODEL:OFF_GATE + 2 * D_MODEL])))
    u = _gelu(uv[:, 0:SGU_DIM])
    vn = _rms(_gelu(uv[:, SGU_DIM:2 * SGU_DIM]), g_sgu_ref[...]).astype(BF16)
    sp_row = lax.broadcasted_iota(jnp.int32, (SGU_CHUNK, SGU_CHUNK), 0) // CHUNK
    sp_col = lax.broadcasted_iota(jnp.int32, (SGU_CHUNK, SGU_CHUNK), 1) // CHUNK
    w_sp = [jnp.where(sp_col <= sp_row, w_sp_ref[g], 0.0).astype(BF16) for g in range(SGU_GROUPS)]
    for c in range(tb // SGU_CHUNK):
        rows = slice(c * SGU_CHUNK, (c + 1) * SGU_CHUNK)
        for g in range(SGU_GROUPS):
            cols = slice(g * SGU_GROUP_DIM, (g + 1) * SGU_GROUP_DIM)
            sp = _dot(w_sp[g], vn[rows, cols]) + b_sp_ref[:, cols]
            yb_ref[rows, cols] = (u[rows, cols] * sp).astype(BF16)

    gates.append(_sigmoid(_dot(hb, w_in_ref[:, OFF_GATE + 2 * D_MODEL:OFF_GATE + 3 * D_MODEL])))
    cbuf[POOL_HALO:POOL_HALO + tb, :] = pc
    t_pos = seq_pos0 + lax.broadcasted_iota(jnp.int32, (tb, POOL_GROUP_DIM), 0)
    for g, w in enumerate(POOL_WINDOWS):
        cols = slice(g * POOL_GROUP_DIM, (g + 1) * POOL_GROUP_DIM)
        acc = cbuf[:, cols]
        lag = 1
        while lag < w:
            acc = acc + pltpu.roll(acc, lag, 0)
            lag *= 2
        cnt = jnp.minimum(t_pos + 1, w).astype(F32)
        pooled = (acc[POOL_HALO:, :] / cnt - pc[:, cols]).astype(BF16)
        mixed = _dot(pooled, w_pool_ref[cols, :]) * pool_scale_ref[:, cols]
        yc_ref[:, cols] = mixed.astype(BF16)
    cbuf[0:POOL_HALO, :] = pc[tb - POOL_HALO:tb, :]

    merged = []
    for r in range(2):
        rows = slice(r * half, (r + 1) * half)
        acc = None
        for n, y_ref in enumerate((ya_ref, yb_ref, yc_ref)):
            proj = _dot(y_ref[rows, :], w_branch_ref[n * BRANCH_DIM:(n + 1) * BRANCH_DIM, :])
            term = gates[n][rows] * proj
            acc = term if acc is None else acc + term
        merged.append(acc.astype(BF16))
    merged = jnp.concatenate(merged, axis=0)
    halves = [slice(r * half, (r + 1) * half) for r in range(2)]
    x_mid = [x_ref[rows, :] + _rms(_dot(merged[rows], w_out_ref[...]), g_ref[1:2, :]) for rows in halves]

    ones = jnp.ones((N_MEM, MEM_HEAD_DIM), BF16)
    v_aug = [jnp.concatenate([v_ref[:, h * MEM_HEAD_DIM:(h + 1) * MEM_HEAD_DIM], ones], axis=1)
             for h in range(MEM_HEADS)]

    def mem_scores(x1):
        hm = _rms(x1, g_ref[2:3, :]).astype(BF16)
        qm = (_dot(hm, w_q_ref[...]) * (LOG2_E / math.sqrt(MEM_HEAD_DIM))).astype(BF16)
        return [_dot_nt(qm[:, h * MEM_HEAD_DIM:(h + 1) * MEM_HEAD_DIM],
                        k_ref[:, h * MEM_HEAD_DIM:(h + 1) * MEM_HEAD_DIM]) for h in range(MEM_HEADS)]

    def mem_attend(rows, x1, s_heads):
        for h, s in enumerate(s_heads):
            cols = slice(h * MEM_HEAD_DIM, (h + 1) * MEM_HEAD_DIM)
            m = jnp.max(s, axis=-1, keepdims=True)
            o_aug = _dot(jnp.exp2(s - m).astype(BF16), v_aug[h])
            om_ref[rows, cols] = (o_aug[:, :MEM_HEAD_DIM] / o_aug[:, MEM_HEAD_DIM:]).astype(BF16)
        ym = _dot(om_ref[rows, :], w_o_ref[...])
        o_ref[rows, :] = x1 + _rms(ym, g_ref[3:4, :])

    s_mem = [mem_scores(x1) for x1 in x_mid]
    mem_attend(halves[0], x_mid[0], s_mem[0])
    norm_and_qkv(xn_ref, halves[0])
    mem_attend(halves[1], x_mid[1], s_mem[1])
    norm_and_qkv(xn_ref, halves[1])


def _mixer(layer, x, sinks, g_norm, w_in, w_sp, b_sp_full, g_sgu, w_pool, pool_scale, w_branch, w_out,
           w_q, k_mem, v_mem, w_o, to_cast, seq_len):
    t, d = x.shape
    tb = TOKEN_BLOCK
    assert (tb // ATT_BLOCK) % (D_MODEL // GATE_COLS) == 0 and tb % 2 == 0
    n_steps = t // tb
    bps = seq_len // tb
    kv_spec = pl.BlockSpec((None, None, N_MEM, MEM_DIM), lambda i: (layer, i // bps, 0, 0))
    x_next_spec = pl.BlockSpec((tb, d), lambda i: (jnp.minimum(i + 1, n_steps - 1), 0))
    kern = functools.partial(_mixer_kernel, seq_len // tb, len(to_cast))
    cast_in, cast_out, cast_shapes = _cast_plan(to_cast, layer, n_steps)
    x_spec = pl.BlockSpec((tb, d), lambda i: (i, 0))
    outs = pl.pallas_call(
        kern,
        out_shape=[jax.ShapeDtypeStruct((t, d), F32)] + cast_shapes,
        grid=(n_steps,),
        in_specs=[pl.BlockSpec(memory_space=pltpu.SMEM), x_spec, _layer_spec(g_norm, layer),
                  _whole_spec(w_in), _layer_spec(w_sp, layer), _layer_spec(b_sp_full, layer),
                  _layer_spec(g_sgu, layer), _whole_spec(w_pool), _layer_spec(pool_scale, layer),
                  _whole_spec(w_branch), _whole_spec(w_out), _whole_spec(w_q), kv_spec, kv_spec,
                  _whole_spec(w_o), x_next_spec] + cast_in,
        out_specs=[x_spec] + cast_out,
        scratch_shapes=[
            pltpu.VMEM((ATT_BLOCK + tb, A_KV_DIM), F32),
            pltpu.VMEM((ATT_BLOCK + tb, A_KV_DIM), F32),
            pltpu.VMEM((POOL_HALO + tb, POOL_DIM), F32),
            pltpu.VMEM((tb, BRANCH_DIM), BF16),
            pltpu.VMEM((tb, BRANCH_DIM), BF16),
            pltpu.VMEM((tb, BRANCH_DIM), BF16),
            pltpu.VMEM((tb, MEM_DIM), BF16),
            pltpu.VMEM((tb, D_MODEL), BF16),
            pltpu.VMEM((tb, OFF_SU), F32),
        ],
        compiler_params=pltpu.CompilerParams(
            dimension_semantics=("arbitrary",), vmem_limit_bytes=VMEM_LIMIT_BYTES),
        name="mixer",
    )(sinks, x, g_norm, w_in, w_sp, b_sp_full, g_sgu, w_pool, pool_scale, w_branch, w_out,
      w_q, k_mem, v_mem, w_o, x, *to_cast)
    return outs[0], outs[1:]


def _mem_kv_kernel(n_cast, mem_ref, g_ref, w_ref, *refs):
    cast_src = refs[:n_cast]
    k_ref, v_ref = refs[n_cast:n_cast + 2]
    cast_dst = refs[n_cast + 2:2 * n_cast + 2]
    w_bf = refs[2 * n_cast + 2]
    _cast_rows(cast_src, cast_dst)

    @pl.when(pl.program_id(1) == 0)
    def _():
        w_bf[...] = w_ref[...].astype(BF16)

    mem_n = _rms(mem_ref[...], g_ref[...]).astype(BF16)
    kv = _dot(mem_n, w_bf[...])
    k_ref[...] = kv[:, 0:MEM_DIM].astype(BF16)
    v_ref[...] = kv[:, MEM_DIM:2 * MEM_DIM].astype(BF16)


def _mem_kv(mem, g_mem, w_kv, to_cast):
    b = mem.shape[0]
    out = jax.ShapeDtypeStruct((DEPTH, b, N_MEM, MEM_DIM), BF16)
    kv_spec = pl.BlockSpec((None, None, N_MEM, MEM_DIM), lambda l, i: (l, i, 0, 0))
    cast_in, cast_out, cast_shapes = _cast_plan(to_cast, 0, DEPTH * b, lambda l, i: l * b + i)
    outs = pl.pallas_call(
        functools.partial(_mem_kv_kernel, len(to_cast)),
        out_shape=[out, out] + cast_shapes,
        grid=(DEPTH, b),
        in_specs=[
            pl.BlockSpec((None, N_MEM, D_MODEL), lambda l, i: (i, 0, 0)),
            pl.BlockSpec((None, 1, D_MODEL), lambda l, i: (l, 0, 0)),
            pl.BlockSpec((None, D_MODEL, 2 * MEM_DIM), lambda l, i: (l, 0, 0)),
        ] + cast_in,
        out_specs=[kv_spec, kv_spec] + cast_out,
        scratch_shapes=[pltpu.VMEM((D_MODEL, 2 * MEM_DIM), BF16)],
        compiler_params=pltpu.CompilerParams(
            dimension_semantics=("arbitrary", "arbitrary"), vmem_limit_bytes=VMEM_LIMIT_BYTES),
        name="mem_kv",
    )(mem, g_mem, w_kv, *to_cast)
    return outs[0], outs[1], outs[2:]


def _ffn_kernel(n_cast, x_ref, g_ref, w_up_ref, w_down_ref, *refs):
    cast_src = refs[:n_cast]
    o_ref = refs[n_cast]
    cast_dst = refs[n_cast + 1:]
    rows_per_part = x_ref.shape[0] // FF_PARTS
    parts = [slice(p * rows_per_part, (p + 1) * rows_per_part) for p in range(FF_PARTS)]
    hbs, yfs = [], [None] * FF_PARTS
    for c in range(D_FF // FF_CHUNK):
        cols = slice(c * FF_CHUNK, (c + 1) * FF_CHUNK)
        if c == 1:
            _cast_rows(cast_src, cast_dst)
        for p, rows in enumerate(parts):
            if c == 0:
                hbs.append(_rms(x_ref[rows, :], g_ref[4:5, :]).astype(BF16))
            up = jnp.maximum(_dot(hbs[p], w_up_ref[:, cols]), 0.0)
            part = _dot((up * up).astype(BF16), w_down_ref[cols, :])
            yfs[p] = part if yfs[p] is None else yfs[p] + part
    for p, rows in enumerate(parts):
        o_ref[rows, :] = x_ref[rows, :] + _rms(yfs[p], g_ref[5:6, :])


def _ffn(layer, x, g_norm, w_up, w_down, to_cast, cast_layer):
    t, d = x.shape
    tb = WIDE_BLOCK
    n_steps = t // tb
    cast_in, cast_out, cast_shapes = _cast_plan(to_cast, cast_layer, n_steps)
    x_spec = pl.BlockSpec((tb, d), lambda i: (i, 0))
    outs = pl.pallas_call(
        functools.partial(_ffn_kernel, len(to_cast)),
        out_shape=[jax.ShapeDtypeStruct((t, d), F32)] + cast_shapes,
        grid=(n_steps,),
        in_specs=[x_spec, _layer_spec(g_norm, layer), _whole_spec(w_up), _whole_spec(w_down)] + cast_in,
        out_specs=[x_spec] + cast_out,
        compiler_params=pltpu.CompilerParams(
            dimension_semantics=("arbitrary",), vmem_limit_bytes=VMEM_LIMIT_BYTES),
        name="ffn",
    )(x, g_norm, w_up, w_down, *to_cast)
    return outs[0], outs[1:]


def kernel(x, mem, g_norm, g_mem, w_in, attn_sinks, w_spatial, b_spatial, g_sgu, w_pool, pool_scale,
           w_branch, w_out, w_q_mem, w_kv_mem, w_o_mem, w_up, w_down):
    b, s, d = x.shape
    assert d == D_MODEL and s % TOKEN_BLOCK == 0 and TOKEN_BLOCK % ATT_BLOCK == 0 and s % WIDE_BLOCK == 0
    xt = x.reshape(b * s, d)

    b_sp_full = jnp.repeat(jnp.swapaxes(b_spatial, 1, 2), SGU_GROUP_DIM, axis=2)
    g_sgu3 = g_sgu.reshape(DEPTH, 1, SGU_DIM)
    pool_scale3 = pool_scale.reshape(DEPTH, 1, POOL_DIM)
    mixer_w = (w_in, w_pool.reshape(DEPTH, POOL_DIM, POOL_GROUP_DIM),
               w_branch.reshape(DEPTH, N_BRANCH * BRANCH_DIM, D_MODEL), w_out, w_q_mem, w_o_mem)
    ffn_w = (w_up, w_down)

    k_mem, v_mem, mixer_b = _mem_kv(mem, g_mem.reshape(DEPTH, 1, D_MODEL), w_kv_mem, mixer_w)
    for l in range(DEPTH):
        w_in_b, w_pool_b, w_branch_b, w_out_b, w_q_b, w_o_b = mixer_b
        xt, (w_up_b, w_down_b) = _mixer(
            l, xt, attn_sinks[l], g_norm, w_in_b, w_spatial, b_sp_full, g_sgu3, w_pool_b, pool_scale3,
            w_branch_b, w_out_b, w_q_b, k_mem, v_mem, w_o_b, ffn_w, s)
        xt, mixer_b = _ffn(l, xt, g_norm, w_up_b, w_down_b, mixer_w if l + 1 < DEPTH else (), l + 1)
    return xt.reshape(b, s, d)
```

```python
import functools
import math

import jax
import jax.numpy as jnp
from jax import lax
from jax.experimental import pallas as pl
from jax.experimental.pallas import tpu as pltpu

D_MODEL = 1024
DEPTH = 4
CHUNK = 64
N_MEM = 256
EPS = 1e-6
NEG_INF = -1e30
LOG2_E = math.log2(math.e)
N_NORMS = 6

BRANCH_DIM = D_MODEL // 2
N_BRANCH = 3
HEAD_DIM = 64
A_Q_HEADS = BRANCH_DIM // HEAD_DIM
A_KV_HEADS = 2
A_GROUP = A_Q_HEADS // A_KV_HEADS
A_Q_DIM = A_Q_HEADS * HEAD_DIM
A_KV_DIM = A_KV_HEADS * HEAD_DIM
ATT_BLOCK = 128
SGU_CHUNK = 128
SGU_GROUPS = 4
SGU_DIM = BRANCH_DIM
SGU_GROUP_DIM = SGU_DIM // SGU_GROUPS
POOL_WINDOWS = (2, 4, 8, 16)
POOL_GROUPS = 4
POOL_DIM = BRANCH_DIM
POOL_GROUP_DIM = POOL_DIM // POOL_GROUPS
POOL_HALO = 16
IN_DIM = A_Q_DIM + 2 * A_KV_DIM + 2 * SGU_DIM + POOL_DIM + N_BRANCH * D_MODEL
MEM_HEADS = 4
MEM_HEAD_DIM = 128
MEM_DIM = MEM_HEADS * MEM_HEAD_DIM
D_FF = 4 * D_MODEL

OFF_Q = 0
OFF_K = OFF_Q + A_Q_DIM
OFF_V = OFF_K + A_KV_DIM
OFF_SU = OFF_V + A_KV_DIM
OFF_SV = OFF_SU + SGU_DIM
OFF_PC = OFF_SV + SGU_DIM
OFF_GATE = OFF_PC + POOL_DIM

LANES = 128
BF16_SUBLANES = 16
TOKEN_BLOCK = 512
GATE_COLS = 256
WIDE_BLOCK = 1024
FF_CHUNK = 1024
FF_PARTS = 4
VMEM_LIMIT_BYTES = 56 * 1024 * 1024
N_MIXER_IN = 15

BF16 = jnp.bfloat16
F32 = jnp.float32


def _rms(x, g):
    ms = jnp.mean(x * x, axis=-1, keepdims=True)
    return x * lax.rsqrt(ms + EPS) * g


def _gelu(x):
    c = math.sqrt(2.0 / math.pi)
    t = jnp.tanh(x * (c + (c * 0.044715) * (x * x)))
    hx = 0.5 * x
    return hx + hx * t


def _sigmoid(x):
    return 0.5 * jnp.tanh(0.5 * x) + 0.5


def _dot(a, b):
    return jnp.dot(a, b, preferred_element_type=F32)


def _dot_nt(a, b):
    return lax.dot_general(a, b, (((1,), (1,)), ((), ())), preferred_element_type=F32)


def _layer_spec(stacked, layer):
    tail = (0,) * (stacked.ndim - 1)
    return pl.BlockSpec((None,) + stacked.shape[1:], lambda *_: (layer,) + tail,
                        pipeline_mode=pl.Buffered(1))


def _whole_spec(a):
    zeros = (0,) * a.ndim
    return pl.BlockSpec(a.shape, lambda *_: zeros, pipeline_mode=pl.Buffered(1))


def _cast_plan(stacked_f32, layer, n_steps, step_of=lambda i: i):
    in_specs, out_specs, out_shapes = [], [], []
    for a in stacked_f32:
        r, c = a.shape[1:]
        rb = r // n_steps
        assert rb * n_steps == r and rb % BF16_SUBLANES == 0
        in_specs.append(pl.BlockSpec((None, rb, c), lambda *g: (layer, step_of(*g), 0)))
        out_specs.append(pl.BlockSpec((rb, c), lambda *g: (step_of(*g), 0)))
        out_shapes.append(jax.ShapeDtypeStruct((r, c), BF16))
    return in_specs, out_specs, out_shapes


def _cast_rows(srcs, dsts):
    for src, dst in zip(srcs, dsts):
        dst[...] = src[...].astype(BF16)


def _mixer_kernel(layer, blocks_per_seq, n_cast, *refs):
    (sinks_ref, x_ref, g_ref, w_in_ref, w_sp_ref, b_sp_ref, g_sgu_ref, w_pool_ref, pool_scale_ref,
     w_branch_ref, w_out_ref, w_q_ref, k_ref, v_ref, w_o_ref) = refs[:N_MIXER_IN]
    cast_src = refs[N_MIXER_IN:N_MIXER_IN + n_cast]
    o_ref = refs[N_MIXER_IN + n_cast]
    cast_dst = refs[N_MIXER_IN + n_cast + 1:N_MIXER_IN + 2 * n_cast + 1]
    kbuf, vbuf, cbuf, ya_ref, yb_ref, yc_ref, om_ref = refs[N_MIXER_IN + 2 * n_cast + 1:]
    tb = x_ref.shape[0]
    n_qblk = tb // ATT_BLOCK
    step = pl.program_id(0)
    first = (step % blocks_per_seq) == 0
    seq_pos0 = (step % blocks_per_seq) * tb
    half = tb // 2
    blocks_per_gate = n_qblk // (D_MODEL // GATE_COLS)

    @pl.when(first)
    def _():
        kbuf[0:ATT_BLOCK, :] = jnp.zeros((ATT_BLOCK, A_KV_DIM), F32)
        vbuf[0:ATT_BLOCK, :] = jnp.zeros((ATT_BLOCK, A_KV_DIM), F32)
        cbuf[0:POOL_HALO, :] = jnp.zeros((POOL_HALO, POOL_DIM), F32)

    hb_halves = [_rms(x_ref[r * half:(r + 1) * half, :], g_ref[0:1, :]).astype(BF16) for r in range(2)]
    qkv = jnp.concatenate([_dot(hb_r, w_in_ref[:, OFF_Q:OFF_SU]) for hb_r in hb_halves], axis=0)
    hb = jnp.concatenate(hb_halves, axis=0)

    q = qkv[:, 0:A_Q_DIM] * (LOG2_E / math.sqrt(HEAD_DIM))
    k_new = qkv[:, A_Q_DIM:A_Q_DIM + A_KV_DIM]
    v_new = qkv[:, A_Q_DIM + A_KV_DIM:A_Q_DIM + 2 * A_KV_DIM]
    kbuf[ATT_BLOCK:ATT_BLOCK + tb, :] = k_new
    vbuf[ATT_BLOCK:ATT_BLOCK + tb, :] = v_new
    k_all = kbuf[...]
    v_all = vbuf[...]
    k_sw = pltpu.roll(k_all, HEAD_DIM, 1)
    lo_kv = lax.broadcasted_iota(jnp.int32, k_all.shape, 1) < HEAD_DIM
    k_dup = (jnp.where(lo_kv, k_all, k_sw).astype(BF16), jnp.where(lo_kv, k_sw, k_all).astype(BF16))
    v_t = v_all.T.astype(BF16)
    ones_rows = jnp.ones((BF16_SUBLANES, 2 * ATT_BLOCK), BF16)
    kbuf[0:ATT_BLOCK, :] = k_new[tb - ATT_BLOCK:tb, :]
    vbuf[0:ATT_BLOCK, :] = v_new[tb - ATT_BLOCK:tb, :]

    lo_q = lax.broadcasted_iota(jnp.int32, (ATT_BLOCK, LANES), 1) < HEAD_DIM
    k_row = lax.broadcasted_iota(jnp.int32, (2 * ATT_BLOCK, ATT_BLOCK), 0)
    k_chunk = k_row // CHUNK
    q_chunk = lax.broadcasted_iota(jnp.int32, (2 * ATT_BLOCK, ATT_BLOCK), 1) // CHUNK
    band = (k_chunk >= q_chunk) & (k_chunk <= q_chunk + 2)
    k_min = jnp.where(first, ATT_BLOCK, 0)

    def scores(j):
        rows = slice(j * ATT_BLOCK, (j + 1) * ATT_BLOCK)
        keys = slice(j * ATT_BLOCK, (j + 2) * ATT_BLOCK)
        out = []
        for h in range(A_KV_HEADS):
            q_stack = []
            for p in (2 * h, 2 * h + 1):
                qp = q[rows, p * LANES:(p + 1) * LANES]
                q_stack.append(jnp.where(lo_q, qp, 0.0).astype(BF16))
                q_stack.append(jnp.where(lo_q, 0.0, qp).astype(BF16))
            out.append(_dot_nt(k_dup[h][keys], jnp.concatenate(q_stack, axis=0)))
        return out

    def attend(j, s_heads):
        rows = slice(j * ATT_BLOCK, (j + 1) * ATT_BLOCK)
        keys = slice(j * ATT_BLOCK, (j + 2) * ATT_BLOCK)
        valid = (band & (k_row >= k_min)) if j == 0 else band
        for h in range(A_KV_HEADS):
            probs, sink_w = [], []
            for a in range(A_GROUP):
                sink = sinks_ref[layer, h * A_GROUP + a] * LOG2_E
                s = jnp.where(valid, s_heads[h][:, a * ATT_BLOCK:(a + 1) * ATT_BLOCK], NEG_INF)
                m = jnp.maximum(jnp.max(s, axis=0, keepdims=True), sink)
                probs.append(jnp.exp2(s - m).astype(BF16))
                sink_w.append(jnp.exp2(sink - m))
            v_aug = jnp.concatenate([v_t[h * HEAD_DIM:(h + 1) * HEAD_DIM, keys], ones_rows], axis=0)
            o_aug = _dot(v_aug, jnp.concatenate(probs, axis=1))
            denom = o_aug[HEAD_DIM:HEAD_DIM + 1, :] + jnp.concatenate(sink_w, axis=1)
            o_t = o_aug[0:HEAD_DIM, :] * (1.0 / denom)
            for pi in range(2):
                pair_t = jnp.concatenate(
                    [o_t[:, (2 * pi + i) * ATT_BLOCK:(2 * pi + i + 1) * ATT_BLOCK] for i in range(2)], axis=0)
                p = 2 * h + pi
                ya_ref[rows, p * LANES:(p + 1) * LANES] = pair_t.T.astype(BF16)

    def gate0_chunk(c):
        c0 = OFF_GATE + c * GATE_COLS
        return _dot(hb, w_in_ref[:, c0:c0 + GATE_COLS])

    s_cur = scores(0)
    gate0 = [gate0_chunk(0)]
    for j in range(n_qblk):
        s_next = scores(j + 1) if j + 1 < n_qblk else None
        if (j + 1) % blocks_per_gate == 0 and (j + 1) // blocks_per_gate < D_MODEL // GATE_COLS:
            gate0.append(gate0_chunk((j + 1) // blocks_per_gate))
        attend(j, s_cur)
        s_cur = s_next
    gates = [_sigmoid(jnp.concatenate(gate0, axis=1))]

    _cast_rows(cast_src, cast_dst)

    uv = _dot(hb, w_in_ref[:, OFF_SU:OFF_PC])
    pc = _dot(hb, w_in_ref[:, OFF_PC:OFF_GATE])
    gates.append(_sigmoid(_dot(hb, w_in_ref[:, OFF_GATE + D_MODEL:OFF_GATE + 2 * D_MODEL])))
    u = _gelu(uv[:, 0:SGU_DIM])
    vn = _rms(_gelu(uv[:, SGU_DIM:2 * SGU_DIM]), g_sgu_ref[layer:layer + 1, :]).astype(BF16)
    sp_row = lax.broadcasted_iota(jnp.int32, (SGU_CHUNK, SGU_CHUNK), 0) // CHUNK
    sp_col = lax.broadcasted_iota(jnp.int32, (SGU_CHUNK, SGU_CHUNK), 1) // CHUNK
    w_sp = [jnp.where(sp_col <= sp_row, w_sp_ref[g], 0.0).astype(BF16) for g in range(SGU_GROUPS)]
    for c in range(tb // SGU_CHUNK):
        rows = slice(c * SGU_CHUNK, (c + 1) * SGU_CHUNK)
        for g in range(SGU_GROUPS):
            cols = slice(g * SGU_GROUP_DIM, (g + 1) * SGU_GROUP_DIM)
            sp = _dot(w_sp[g], vn[rows, cols]) + b_sp_ref[:, cols]
            yb_ref[rows, cols] = (u[rows, cols] * sp).astype(BF16)

    gates.append(_sigmoid(_dot(hb, w_in_ref[:, OFF_GATE + 2 * D_MODEL:OFF_GATE + 3 * D_MODEL])))
    cbuf[POOL_HALO:POOL_HALO + tb, :] = pc
    t_pos = seq_pos0 + lax.broadcasted_iota(jnp.int32, (tb, POOL_GROUP_DIM), 0)
    for g, w in enumerate(POOL_WINDOWS):
        cols = slice(g * POOL_GROUP_DIM, (g + 1) * POOL_GROUP_DIM)
        acc = cbuf[:, cols]
        lag = 1
        while lag < w:
            acc = acc + pltpu.roll(acc, lag, 0)
            lag *= 2
        cnt = jnp.minimum(t_pos + 1, w).astype(F32)
        pooled = (acc[POOL_HALO:, :] / cnt - pc[:, cols]).astype(BF16)
        mixed = _dot(pooled, w_pool_ref[cols, :]) * pool_scale_ref[layer:layer + 1, cols]
        yc_ref[:, cols] = mixed.astype(BF16)
    cbuf[0:POOL_HALO, :] = pc[tb - POOL_HALO:tb, :]

    merged = []
    for r in range(2):
        rows = slice(r * half, (r + 1) * half)
        acc = None
        for n, y_ref in enumerate((ya_ref, yb_ref, yc_ref)):
            proj = _dot(y_ref[rows, :], w_branch_ref[n * BRANCH_DIM:(n + 1) * BRANCH_DIM, :])
            term = gates[n][rows] * proj
            acc = term if acc is None else acc + term
        merged.append(acc.astype(BF16))
    merged = jnp.concatenate(merged, axis=0)
    halves = [slice(r * half, (r + 1) * half) for r in range(2)]
    x_mid = [x_ref[rows, :] + _rms(_dot(merged[rows], w_out_ref[...]), g_ref[1:2, :]) for rows in halves]

    ones = jnp.ones((N_MEM, MEM_HEAD_DIM), BF16)
    v_aug = [jnp.concatenate([v_ref[:, h * MEM_HEAD_DIM:(h + 1) * MEM_HEAD_DIM], ones], axis=1)
             for h in range(MEM_HEADS)]

    def mem_scores(x1):
        hm = _rms(x1, g_ref[2:3, :]).astype(BF16)
        qm = (_dot(hm, w_q_ref[...]) * (LOG2_E / math.sqrt(MEM_HEAD_DIM))).astype(BF16)
        return [_dot_nt(qm[:, h * MEM_HEAD_DIM:(h + 1) * MEM_HEAD_DIM],
                        k_ref[:, h * MEM_HEAD_DIM:(h + 1) * MEM_HEAD_DIM]) for h in range(MEM_HEADS)]

    def mem_attend(rows, x1, s_heads):
        for h, s in enumerate(s_heads):
            cols = slice(h * MEM_HEAD_DIM, (h + 1) * MEM_HEAD_DIM)
            m = jnp.max(s, axis=-1, keepdims=True)
            o_aug = _dot(jnp.exp2(s - m).astype(BF16), v_aug[h])
            om_ref[rows, cols] = (o_aug[:, :MEM_HEAD_DIM] / o_aug[:, MEM_HEAD_DIM:]).astype(BF16)
        ym = _dot(om_ref[rows, :], w_o_ref[...])
        o_ref[rows, :] = x1 + _rms(ym, g_ref[3:4, :])

    s_mem = [mem_scores(x1) for x1 in x_mid]
    for rows, x1, s_heads in zip(halves, x_mid, s_mem):
        mem_attend(rows, x1, s_heads)


def _mixer(layer, x, sinks, g_norm, w_in, w_sp, b_sp_full, g_sgu, w_pool, pool_scale, w_branch, w_out,
           w_q, k_mem, v_mem, w_o, to_cast, seq_len):
    t, d = x.shape
    tb = TOKEN_BLOCK
    assert (tb // ATT_BLOCK) % (D_MODEL // GATE_COLS) == 0 and tb % 2 == 0
    n_steps = t // tb
    bps = seq_len // tb
    kv_spec = pl.BlockSpec((None, N_MEM, MEM_DIM), lambda i: (layer, i // bps, 0))
    kern = functools.partial(_mixer_kernel, layer, seq_len // tb, len(to_cast))
    cast_in, cast_out, cast_shapes = _cast_plan(to_cast, layer, n_steps)
    x_spec = pl.BlockSpec((tb, d), lambda i: (i, 0))
    outs = pl.pallas_call(
        kern,
        out_shape=[jax.ShapeDtypeStruct((t, d), F32)] + cast_shapes,
        grid=(n_steps,),
        in_specs=[pl.BlockSpec(memory_space=pltpu.SMEM), x_spec, _layer_spec(g_norm, layer),
                  _whole_spec(w_in), _layer_spec(w_sp, layer), _layer_spec(b_sp_full, layer),
                  _whole_spec(g_sgu), _whole_spec(w_pool), _whole_spec(pool_scale),
                  _whole_spec(w_branch), _whole_spec(w_out), _whole_spec(w_q), kv_spec, kv_spec,
                  _whole_spec(w_o)] + cast_in,
        out_specs=[x_spec] + cast_out,
        scratch_shapes=[
            pltpu.VMEM((ATT_BLOCK + tb, A_KV_DIM), F32),
            pltpu.VMEM((ATT_BLOCK + tb, A_KV_DIM), F32),
            pltpu.VMEM((POOL_HALO + tb, POOL_DIM), F32),
            pltpu.VMEM((tb, BRANCH_DIM), BF16),
            pltpu.VMEM((tb, BRANCH_DIM), BF16),
            pltpu.VMEM((tb, BRANCH_DIM), BF16),
            pltpu.VMEM((tb, MEM_DIM), BF16),
        ],
        compiler_params=pltpu.CompilerParams(
            dimension_semantics=("arbitrary",), vmem_limit_bytes=VMEM_LIMIT_BYTES),
        name="mixer",
    )(sinks, x, g_norm, w_in, w_sp, b_sp_full, g_sgu, w_pool, pool_scale, w_branch, w_out,
      w_q, k_mem, v_mem, w_o, *to_cast)
    return outs[0], outs[1:]


def _mem_kv_kernel(n_cast, mem_ref, g_ref, w_ref, *refs):
    cast_src = refs[:n_cast]
    k_ref, v_ref = refs[n_cast:n_cast + 2]
    cast_dst = refs[n_cast + 2:2 * n_cast + 2]
    w_bf = refs[2 * n_cast + 2]
    _cast_rows(cast_src, cast_dst)
    w_bf[...] = w_ref[...].astype(BF16)
    layer = pl.program_id(0)
    mem_n = _rms(mem_ref[...], g_ref[pl.ds(layer, 1), :]).astype(BF16)
    kv = _dot(mem_n, w_bf[...])
    k_ref[...] = kv[:, 0:MEM_DIM].astype(BF16)
    v_ref[...] = kv[:, MEM_DIM:2 * MEM_DIM].astype(BF16)


def _mem_kv(mem, g_mem, w_kv, to_cast):
    rows = mem.shape[0]
    out = jax.ShapeDtypeStruct((DEPTH, rows, MEM_DIM), BF16)
    kv_spec = pl.BlockSpec((None, rows, MEM_DIM), lambda l: (l, 0, 0))
    cast_in, cast_out, cast_shapes = _cast_plan(to_cast, 0, DEPTH)
    outs = pl.pallas_call(
        functools.partial(_mem_kv_kernel, len(to_cast)),
        out_shape=[out, out] + cast_shapes,
        grid=(DEPTH,),
        in_specs=[
            _whole_spec(mem),
            _whole_spec(g_mem),
            pl.BlockSpec((None, D_MODEL, 2 * MEM_DIM), lambda l: (l, 0, 0)),
        ] + cast_in,
        out_specs=[kv_spec, kv_spec] + cast_out,
        scratch_shapes=[pltpu.VMEM((D_MODEL, 2 * MEM_DIM), BF16)],
        compiler_params=pltpu.CompilerParams(
            dimension_semantics=("arbitrary",), vmem_limit_bytes=VMEM_LIMIT_BYTES),
        name="mem_kv",
    )(mem, g_mem, w_kv, *to_cast)
    return outs[0], outs[1], outs[2:]


def _ffn_kernel(n_cast, x_ref, g_ref, w_up_ref, w_down_ref, *refs):
    cast_src = refs[:n_cast]
    o_ref = refs[n_cast]
    cast_dst = refs[n_cast + 1:]
    rows_per_part = x_ref.shape[0] // FF_PARTS
    parts = [slice(p * rows_per_part, (p + 1) * rows_per_part) for p in range(FF_PARTS)]
    hbs, yfs = [], [None] * FF_PARTS
    for c in range(D_FF // FF_CHUNK):
        cols = slice(c * FF_CHUNK, (c + 1) * FF_CHUNK)
        if c == 1:
            _cast_rows(cast_src, cast_dst)
        for p, rows in enumerate(parts):
            if c == 0:
                hbs.append(_rms(x_ref[rows, :], g_ref[4:5, :]).astype(BF16))
            up = jnp.maximum(_dot(hbs[p], w_up_ref[:, cols]), 0.0)
            part = _dot((up * up).astype(BF16), w_down_ref[cols, :])
            yfs[p] = part if yfs[p] is None else yfs[p] + part
    for p, rows in enumerate(parts):
        o_ref[rows, :] = x_ref[rows, :] + _rms(yfs[p], g_ref[5:6, :])


def _ffn(layer, x, g_norm, w_up, w_down, to_cast, cast_layer):
    t, d = x.shape
    tb = WIDE_BLOCK
    n_steps = t // tb
    cast_in, cast_out, cast_shapes = _cast_plan(to_cast, cast_layer, n_steps)
    x_spec = pl.BlockSpec((tb, d), lambda i: (i, 0))
    outs = pl.pallas_call(
        functools.partial(_ffn_kernel, len(to_cast)),
        out_shape=[jax.ShapeDtypeStruct((t, d), F32)] + cast_shapes,
        grid=(n_steps,),
        in_specs=[x_spec, _layer_spec(g_norm, layer), _whole_spec(w_up), _whole_spec(w_down)] + cast_in,
        out_specs=[x_spec] + cast_out,
        compiler_params=pltpu.CompilerParams(
            dimension_semantics=("arbitrary",), vmem_limit_bytes=VMEM_LIMIT_BYTES),
        name="ffn",
    )(x, g_norm, w_up, w_down, *to_cast)
    return outs[0], outs[1:]


def kernel(x, mem, g_norm, g_mem, w_in, attn_sinks, w_spatial, b_spatial, g_sgu, w_pool, pool_scale,
           w_branch, w_out, w_q_mem, w_kv_mem, w_o_mem, w_up, w_down):
    b, s, d = x.shape
    assert d == D_MODEL and s % TOKEN_BLOCK == 0 and TOKEN_BLOCK % ATT_BLOCK == 0 and s % WIDE_BLOCK == 0
    xt = x.reshape(b * s, d)

    b_sp_full = jnp.repeat(jnp.swapaxes(b_spatial, 1, 2), SGU_GROUP_DIM, axis=2)
    mixer_w = (w_in, w_pool.reshape(DEPTH, POOL_DIM, POOL_GROUP_DIM),
               w_branch.reshape(DEPTH, N_BRANCH * BRANCH_DIM, D_MODEL), w_out, w_q_mem, w_o_mem)
    ffn_w = (w_up, w_down)

    k_mem, v_mem, mixer_b = _mem_kv(mem.reshape(b * N_MEM, D_MODEL), g_mem, w_kv_mem, mixer_w)
    for l in range(DEPTH):
        w_in_b, w_pool_b, w_branch_b, w_out_b, w_q_b, w_o_b = mixer_b
        xt, (w_up_b, w_down_b) = _mixer(
            l, xt, attn_sinks, g_norm, w_in_b, w_spatial, b_sp_full, g_sgu, w_pool_b, pool_scale,
            w_branch_b, w_out_b, w_q_b, k_mem, v_mem, w_o_b, ffn_w, s)
        xt, mixer_b = _ffn(l, xt, g_norm, w_up_b, w_down_b, mixer_w if l + 1 < DEPTH else (), l + 1)
    return xt.reshape(b, s, d)
```
